```python
import math
import jax, jax.numpy as jnp
from jax import lax
import numpy as np

D_MODEL = 1024
BATCH = 4
SEQ = 8192
DEPTH = 2
DEC_BATCH = 32
DEC_SEQ = 1
PAST_LEN = 16384
PAGE_SIZE = 128

N_META = 16
N_MIXERS = 2
N_HEADS = 8
HEAD_DIM = 64
V_DIM = 2 * HEAD_DIM
Q_WIDTH = N_HEADS * 2 * HEAD_DIM
N_ATTN_LAYERS = (DEPTH + N_MIXERS - 1) // N_MIXERS
N_POOL_LAYERS = DEPTH // N_MIXERS
POOL_WINDOWS = (2, 4, 8, 16)
POOL_GROUP = D_MODEL // len(POOL_WINDOWS)
POOL_CTX = max(POOL_WINDOWS) - 1
D_FF = 4 * D_MODEL
Q_BLOCK = 128
NORM_EPS = 1e-6

kernel_name = 'meta_diffattn_pool_hybrid_step'


def _rmsnorm(x, g):
    xf = x.astype(jnp.float32)
    y = xf * lax.rsqrt(jnp.mean(xf * xf, axis=-1, keepdims=True) + NORM_EPS)
    return (y * g.astype(jnp.float32)).astype(x.dtype)


def _alibi_slopes():
    start = 2.0 ** (-8.0 / N_HEADS)
    return jnp.asarray(start ** np.arange(1, N_HEADS + 1), dtype=jnp.float32)


def _lambda_init(layer):
    return 0.8 - 0.6 * math.exp(-0.3 * layer)


def _diff_lambda(lq1, lk1, lq2, lk2, layer):
    f = lambda a: a.astype(jnp.float32)
    return (jnp.exp(jnp.sum(f(lq1) * f(lk1))) - jnp.exp(jnp.sum(f(lq2) * f(lk2)))
            + _lambda_init(layer))


def _diff_qkv(h, w_qkv, q_g, k_g):
    b, t, _ = h.shape
    qkv = h @ w_qkv
    q = qkv[..., :Q_WIDTH].reshape(b, t, N_HEADS, 2, HEAD_DIM)
    k = qkv[..., Q_WIDTH:2 * Q_WIDTH].reshape(b, t, N_HEADS, 2, HEAD_DIM)
    v = qkv[..., 2 * Q_WIDTH:].reshape(b, t, N_HEADS, V_DIM)
    q = _rmsnorm(q, q_g) * (HEAD_DIM ** -0.5)
    k = _rmsnorm(k, k_g)
    return q, k, v


def _diff_attend(q, k, v, q_pos, k_pos, lam):
    s = jnp.einsum('bqhcd,bkhcd->bhcqk', q, k, preferred_element_type=jnp.float32)
    dist = (q_pos[:, None] - k_pos[None, :]).astype(jnp.float32)
    bias = -_alibi_slopes()[:, None, None] * dist
    s = jnp.where((dist >= 0)[None, None, None], s + bias[None, :, None], -jnp.inf)
    p = jax.nn.softmax(s, axis=-1)
    w = p[:, :, 0] - lam * p[:, :, 1]
    return jnp.einsum('bhqk,bkhe->bqhe', w, v.astype(jnp.float32))


def _diff_out(o, sub_g, w_o, layer):
    b, t = o.shape[:2]
    o = _rmsnorm(o, sub_g) * (1.0 - _lambda_init(layer))
    return o.reshape(b, t, Q_WIDTH).astype(w_o.dtype) @ w_o


def _attn_prompt(h, w_qkv, q_g, k_g, lam, sub_g, w_o, layer):
    q, k, v = _diff_qkv(h, w_qkv, q_g, k_g)
    t = h.shape[1]
    pos = jnp.arange(t, dtype=jnp.int32)
    bounds = [0, N_META] + list(range(N_META + Q_BLOCK, t + 1, Q_BLOCK))
    if bounds[-1] != t:
        bounds.append(t)
    outs = [_diff_attend(q[:, s:e], k[:, :e], v[:, :e], pos[s:e], pos[:e], lam)
            for s, e in zip(bounds[:-1], bounds[1:])]
    o = jnp.concatenate(outs, axis=1)
    return _diff_out(o, sub_g, w_o, layer), k, v


def _attn_sample(h, cache_k, cache_v, a, page_table, w_qkv, q_g, k_g, lam, sub_g, w_o, layer):
    q, k_new, v_new = _diff_qkv(h, w_qkv, q_g, k_g)
    db, tn = h.shape[:2]
    past = page_table.shape[1] * PAGE_SIZE
    k_past = cache_k[a, page_table].reshape(db, past, N_HEADS, 2, HEAD_DIM)
    v_past = cache_v[a, page_table].reshape(db, past, N_HEADS, V_DIM)
    k_all = jnp.concatenate([k_past.astype(k_new.dtype), k_new], axis=1)
    v_all = jnp.concatenate([v_past.astype(v_new.dtype), v_new], axis=1)
    k_pos = jnp.arange(past + tn, dtype=jnp.int32)
    q_pos = past + jnp.arange(tn, dtype=jnp.int32)
    o = _diff_attend(q, k_all, v_all, q_pos, k_pos, lam)
    return _diff_out(o, sub_g, w_o, layer), k_new, v_new


def _pool_mix(h_ext, start_pos, w_pool, scale):
    b, l, d = h_ext.shape
    t = l - POOL_CTX
    hf = h_ext.astype(jnp.float32)
    csum = jnp.concatenate([jnp.zeros((b, 1, d), jnp.float32), jnp.cumsum(hf, axis=1)], axis=1)
    pos = start_pos + jnp.arange(t, dtype=jnp.int32)
    h_new = hf[:, POOL_CTX:]
    outs = []
    for g, w in enumerate(POOL_WINDOWS):
        c0, c1 = g * POOL_GROUP, (g + 1) * POOL_GROUP
        win = (csum[:, POOL_CTX + 1:POOL_CTX + 1 + t, c0:c1]
               - csum[:, POOL_CTX + 1 - w:POOL_CTX + 1 - w + t, c0:c1])
        mean = win / jnp.minimum(pos + 1, w).astype(jnp.float32)[None, :, None]
        outs.append(jnp.einsum('btc,cd->btd', (mean - h_new[..., c0:c1]).astype(w_pool.dtype), w_pool[g]))
    y = jnp.concatenate(outs, axis=-1) * scale
    return y.astype(h_ext.dtype), h_ext[:, -POOL_CTX:]


def _sqrelu_mlp(h, w_up, w_down):
    return jnp.square(jax.nn.relu(h @ w_up)) @ w_down


def setup_inputs(seed: int = 0) -> dict:
    key = jax.random.key(seed)
    ks = jax.random.split(key, 24)
    nrm = jax.random.normal
    f32 = jnp.float32
    n_pages = PAST_LEN // PAGE_SIZE
    n_used = DEC_BATCH * n_pages
    n_phys = n_used + n_used // 4
    return {
        'x_prompt': nrm(ks[0], (BATCH, SEQ, D_MODEL), f32),
        'x_sample': nrm(ks[1], (DEC_BATCH, DEC_SEQ, D_MODEL), f32),
        'cache_k': nrm(ks[2], (N_ATTN_LAYERS, n_phys, PAGE_SIZE, N_HEADS, 2, HEAD_DIM), f32),
        'cache_v': nrm(ks[3], (N_ATTN_LAYERS, n_phys, PAGE_SIZE, N_HEADS, V_DIM), f32),
        'page_table': jax.random.permutation(ks[4], n_phys)[:n_used].reshape(DEC_BATCH, n_pages).astype(jnp.int32),
        'state_pool': nrm(ks[5], (N_POOL_LAYERS, DEC_BATCH, POOL_CTX, D_MODEL), f32),
        'meta_tokens': nrm(ks[6], (N_META, D_MODEL), f32),
        'norm_mix_g': 1.0 + 0.02 * nrm(ks[7], (DEPTH, D_MODEL), f32),
        'norm_ffn_g': 1.0 + 0.02 * nrm(ks[8], (DEPTH, D_MODEL), f32),
        'w_qkv': nrm(ks[9], (N_ATTN_LAYERS, D_MODEL, 3 * Q_WIDTH), f32) * D_MODEL ** -0.5,
        'q_norm_g': 1.0 + 0.02 * nrm(ks[10], (N_ATTN_LAYERS, HEAD_DIM), f32),
        'k_norm_g': 1.0 + 0.02 * nrm(ks[11], (N_ATTN_LAYERS, HEAD_DIM), f32),
        'lambda_q1': 0.1 * nrm(ks[12], (N_ATTN_LAYERS, HEAD_DIM), f32),
        'lambda_k1': 0.1 * nrm(ks[13], (N_ATTN_LAYERS, HEAD_DIM), f32),
        'lambda_q2': 0.1 * nrm(ks[14], (N_ATTN_LAYERS, HEAD_DIM), f32),
        'lambda_k2': 0.1 * nrm(ks[15], (N_ATTN_LAYERS, HEAD_DIM), f32),
        'subln_g': 1.0 + 0.02 * nrm(ks[16], (N_ATTN_LAYERS, V_DIM), f32),
        'w_o': nrm(ks[17], (N_ATTN_LAYERS, Q_WIDTH, D_MODEL), f32) * Q_WIDTH ** -0.5,
        'pool_w': nrm(ks[18], (N_POOL_LAYERS, len(POOL_WINDOWS), POOL_GROUP, POOL_GROUP), f32) * POOL_GROUP ** -0.5,
        'pool_scale': 1.0 + 0.02 * nrm(ks[19], (N_POOL_LAYERS, D_MODEL), f32),
        'w_up': nrm(ks[20], (DEPTH, D_MODEL, D_FF), f32) * D_MODEL ** -0.5,
        'w_down': nrm(ks[21], (DEPTH, D_FF, D_MODEL), f32) * D_FF ** -0.5,
    }


def reference(x_prompt, x_sample, cache_k, cache_v, page_table, state_pool, meta_tokens,
              norm_mix_g, norm_ffn_g, w_qkv, q_norm_g, k_norm_g, lambda_q1, lambda_k1,
              lambda_q2, lambda_k2, subln_g, w_o, pool_w, pool_scale, w_up, w_down):
    b = x_prompt.shape[0]
    meta = jnp.broadcast_to(meta_tokens.astype(x_prompt.dtype)[None], (b, N_META, D_MODEL))
    xp = jnp.concatenate([meta, x_prompt], axis=1)
    xs = x_sample
    past = page_table.shape[1] * PAGE_SIZE
    kp_l, vp_l, ks_l, vs_l, sp_l, ss_l = [], [], [], [], [], []
    for i in range(DEPTH):
        hp = _rmsnorm(xp, norm_mix_g[i])
        hs = _rmsnorm(xs, norm_mix_g[i])
        if i % N_MIXERS == 0:
            a = i // N_MIXERS
            lam = _diff_lambda(lambda_q1[a], lambda_k1[a], lambda_q2[a], lambda_k2[a], i)
            yp, kp, vp = _attn_prompt(hp, w_qkv[a], q_norm_g[a], k_norm_g[a], lam,
                                      subln_g[a], w_o[a], i)
            ys, kn, vn = _attn_sample(hs, cache_k, cache_v, a, page_table, w_qkv[a], q_norm_g[a],
                                      k_norm_g[a], lam, subln_g[a], w_o[a], i)
            kp_l.append(kp); vp_l.append(vp); ks_l.append(kn); vs_l.append(vn)
        else:
            p = i // N_MIXERS
            zeros = jnp.zeros((b, POOL_CTX, D_MODEL), hp.dtype)
            yp, sp = _pool_mix(jnp.concatenate([zeros, hp], axis=1), 0, pool_w[p], pool_scale[p])
            ys, sn = _pool_mix(jnp.concatenate([state_pool[p].astype(hs.dtype), hs], axis=1), past,
                               pool_w[p], pool_scale[p])
            sp_l.append(sp); ss_l.append(sn)
        xp = xp + yp
        xs = xs + ys
        xp = xp + _sqrelu_mlp(_rmsnorm(xp, norm_ffn_g[i]), w_up[i], w_down[i])
        xs = xs + _sqrelu_mlp(_rmsnorm(xs, norm_ffn_g[i]), w_up[i], w_down[i])
    y_prompt = xp[:, N_META:]
    return (y_prompt, xs, jnp.stack(kp_l), jnp.stack(vp_l), jnp.stack(ks_l), jnp.stack(vs_l),
            jnp.stack(sp_l), jnp.stack(ss_l))
```

```python
import functools
import math

import numpy as np
import jax
import jax.numpy as jnp
from jax import lax
from jax.experimental import pallas as pl
from jax.experimental.pallas import tpu as pltpu

F32 = jnp.float32
BF16 = jnp.bfloat16

NORM_EPS = 1e-6
POOL_WINDOWS = (2, 4, 8, 16)
POOL_CTX = max(POOL_WINDOWS) - 1
N_MIXERS = 2

MXU_DIM_V7X = 256
ROW_TILE = 768
ATTN_TILE = 768
FF_CHUNK = 1024
DECODE_PAGES_PER_STEP = 8
VMEM_LIMIT_BYTES_V7X = 56 * 1024 * 1024
NEG_BIG = -1e30


def _lambda_init(layer):
    return 0.8 - 0.6 * math.exp(-0.3 * layer)


def _cparams(*sem):
    return pltpu.CompilerParams(dimension_semantics=sem, vmem_limit_bytes=VMEM_LIMIT_BYTES_V7X)


def _const_spec(shape):
    nd = len(shape)
    return pl.BlockSpec(shape, lambda *_: (0,) * nd, pipeline_mode=pl.Buffered(1))


def _rmsnorm_rows(x, g):
    ms = jnp.mean(x * x, axis=-1, keepdims=True)
    return x * lax.rsqrt(ms + NORM_EPS) * g


def _diff_lambda(lq1_ref, lk1_ref, lq2_ref, lk2_ref, layer):
    a = jnp.sum(lq1_ref[...] * lk1_ref[...], axis=-1, keepdims=True)
    b = jnp.sum(lq2_ref[...] * lk2_ref[...], axis=-1, keepdims=True)
    return jnp.exp(a) - jnp.exp(b) + _lambda_init(layer)


def _qkv_body(x_ref, g_ref, wq_ref, wkt_ref, wv_ref, qg_ref, kgc_ref, bd_ref,
              q_ref, ktf_ref, vf_ref, ktb_ref, vb_ref, *, head_dim):
    xn = _rmsnorm_rows(x_ref[0], g_ref[...]).astype(BF16)
    tm = xn.shape[0]
    qw = q_ref.shape[-1]

    q = jnp.dot(xn, wq_ref[...], preferred_element_type=F32)
    qq = (q * q).astype(BF16)
    parts = [jnp.dot(qq[:, c:c + MXU_DIM_V7X], bd_ref[...], preferred_element_type=F32)
             for c in range(0, qw, MXU_DIM_V7X)]
    q = q * lax.rsqrt(jnp.concatenate(parts, axis=-1) + NORM_EPS) * qg_ref[...]
    q_ref[0] = (q * (head_dim ** -0.5)).astype(BF16)

    kt = lax.dot_general(wkt_ref[...], xn, (((1,), (1,)), ((), ())), preferred_element_type=F32)
    kt3 = kt.reshape(qw // head_dim, head_dim, tm)
    ms = jnp.mean(kt3 * kt3, axis=1, keepdims=True)
    kt = (kt3 * lax.rsqrt(ms + NORM_EPS)).reshape(qw, tm) * kgc_ref[...]
    ktf_ref[0] = kt
    ktb_ref[0, :, 0] = kt.astype(BF16).reshape(ktb_ref.shape[1], ktb_ref.shape[3], tm)

    v = jnp.dot(xn, wv_ref[...], preferred_element_type=F32)
    vf_ref[0] = v
    vb_ref[0] = v.astype(BF16)


def _qkv(x, t_valid, g, wq_bf, wkt_bf, wv_bf, qg_t, kg_col, bd, head_dim, tm):
    b, tp, d = x.shape
    qw = wq_bf.shape[1]
    hw = 2 * head_dim
    row = lambda bi, ti: (bi, ti, 0)
    return pl.pallas_call(
        functools.partial(_qkv_body, head_dim=head_dim),
        grid=(b, tp // tm),
        in_specs=[pl.BlockSpec((1, tm, d), row), _const_spec((1, d)),
                  _const_spec(wq_bf.shape), _const_spec(wkt_bf.shape), _const_spec(wv_bf.shape),
                  _const_spec((1, qw)), _const_spec((qw, 1)), _const_spec(bd.shape)],
        out_specs=[pl.BlockSpec((1, tm, qw), row),
                   pl.BlockSpec((1, qw, tm), lambda bi, ti: (bi, 0, ti)),
                   pl.BlockSpec((1, tm, qw), row),
                   pl.BlockSpec((1, qw // hw, 1, hw, tm), lambda bi, ti: (bi, 0, ti, 0, 0)),
                   pl.BlockSpec((1, tm, qw), row)],
        out_shape=[jax.ShapeDtypeStruct((b, tp, qw), BF16),
                   jax.ShapeDtypeStruct((b, qw, t_valid), F32),
                   jax.ShapeDtypeStruct((b, t_valid, qw), F32),
                   jax.ShapeDtypeStruct((b, qw // hw, tp // tm, hw, tm), BF16),
                   jax.ShapeDtypeStruct((b, tp, qw), BF16)],
        compiler_params=_cparams("parallel", "parallel"),
        name="qkv",
    )(x, g, wq_bf, wkt_bf, wv_bf, qg_t, kg_col, bd)


def _attn_body(slopes_ref, q_ref, k_ref, v_ref, lq1_ref, lk1_ref, lq2_ref, lk2_ref, sg_ref, o_ref,
               q2_ref, m_ref, l_ref, acc_ref, *, tq, tk, head_dim, layer):
    h = pl.program_id(1)
    i = pl.program_id(2)
    slope = slopes_ref[h]
    vd = acc_ref.shape[-1]

    q = q_ref[0]
    lane = lax.broadcasted_iota(jnp.int32, q.shape, 1)
    zero = jnp.zeros_like(q)
    q2_ref[0:tq, :] = jnp.where(lane < head_dim, q, zero)
    q2_ref[tq:2 * tq, :] = jnp.where(lane >= head_dim, q, zero)
    m_ref[...] = jnp.full(m_ref.shape, NEG_BIG, F32)
    l_ref[...] = jnp.zeros(l_ref.shape, F32)
    acc_ref[...] = jnp.zeros(acc_ref.shape, F32)

    rel = (lax.broadcasted_iota(jnp.int32, (tq, tk), 1)
           - lax.broadcasted_iota(jnp.int32, (tq, tk), 0))
    rel2 = jnp.concatenate([rel, rel], axis=0)
    rel_bias = rel2.astype(F32) * slope

    def step(j, masked):
        off = j * tk - i * tq
        vj = v_ref[0, pl.ds(pl.multiple_of(j * tk, tk), tk), :]
        s = jnp.dot(q2_ref[...], k_ref[0, 0, j], preferred_element_type=F32)
        t = s + rel_bias
        if masked:
            t = jnp.where(rel2 + off <= 0, t, NEG_BIG)
        c = slope * off.astype(F32)
        m_prev = m_ref[...]
        m_new = jnp.maximum(m_prev, jnp.max(t, axis=-1, keepdims=True) + c)
        alpha = jnp.exp(m_prev - m_new)
        p = jnp.exp(t - (m_new - c))
        l_ref[...] = alpha * l_ref[...] + jnp.sum(p, axis=-1, keepdims=True)
        acc_ref[...] = alpha * acc_ref[...] + jnp.dot(p.astype(BF16), vj, preferred_element_type=F32)
        m_ref[...] = m_new

    n_full = (i * tq + 1) // tk
    n_need = ((i + 1) * tq + tk - 1) // tk

    def full_step(j, carry):
        step(j, False)
        return carry

    def diag_step(j, carry):
        step(j, True)
        return carry

    lax.fori_loop(0, n_full, full_step, 0)
    lax.fori_loop(n_full, n_need, diag_step, 0)

    lam = _diff_lambda(lq1_ref, lk1_ref, lq2_ref, lk2_ref, layer)
    o1 = acc_ref[0:tq, :] / l_ref[0:tq, :]
    o2 = acc_ref[tq:2 * tq, :] / l_ref[tq:2 * tq, :]
    o = o1 - lam * o2
    o = _rmsnorm_rows(o, sg_ref[...]) * (1.0 - _lambda_init(layer))
    o_ref[0] = o.astype(o_ref.dtype)


def _attn_prompt(q_bf, kt_bf, v_bf, slopes, lq1, lk1, lq2, lk2, sg, head_dim, v_dim, layer):
    b, tp, qw = q_bf.shape
    n_heads, n_kt, _, ta = kt_bf.shape[1:]
    assert v_dim == 2 * head_dim and n_kt * ta == tp and qw == n_heads * 2 * head_dim
    small = _const_spec((1, head_dim))
    return pl.pallas_call(
        functools.partial(_attn_body, tq=ta, tk=ta, head_dim=head_dim, layer=layer),
        grid_spec=pltpu.PrefetchScalarGridSpec(
            num_scalar_prefetch=1,
            grid=(b, n_heads, tp // ta),
            in_specs=[pl.BlockSpec((1, ta, 2 * head_dim), lambda bi, hi, qi, sl: (bi, qi, hi)),
                      pl.BlockSpec((1, 1, n_kt, 2 * head_dim, ta), lambda bi, hi, qi, sl: (bi, hi, 0, 0, 0)),
                      pl.BlockSpec((1, tp, v_dim), lambda bi, hi, qi, sl: (bi, 0, hi)),
                      small, small, small, small, _const_spec((1, v_dim))],
            out_specs=pl.BlockSpec((1, ta, v_dim), lambda bi, hi, qi, sl: (bi, qi, hi)),
            scratch_shapes=[pltpu.VMEM((2 * ta, 2 * head_dim), BF16),
                            pltpu.VMEM((2 * ta, 1), F32),
                            pltpu.VMEM((2 * ta, 1), F32),
                            pltpu.VMEM((2 * ta, v_dim), F32)]),
        out_shape=jax.ShapeDtypeStruct((b, tp, n_heads * v_dim), BF16),
        compiler_params=_cparams("parallel", "parallel", "arbitrary"),
        name="attn_prompt",
    )(slopes, q_bf, kt_bf, v_bf, lq1, lk1, lq2, lk2, sg)


def _decode_body(pt_ref, q_ref, kn_ref, vn_ref, slope_ref, lq1_ref, lk1_ref, lq2_ref, lk2_ref, sg_ref,
                 ex_ref, *refs, pages, page, head_dim, past, layer):
    k_refs = refs[:pages]
    v_refs = refs[pages:2 * pages]
    o_ref = refs[2 * pages]
    qt_ref, m_ref, l_ref, acc_ref = refs[2 * pages + 1:]
    p_idx = pl.program_id(1)
    n_maps, width = qt_ref.shape
    n_heads = n_maps // 2

    @pl.when(p_idx == 0)
    def _init():
        q = q_ref[0].astype(F32)
        row = lax.broadcasted_iota(jnp.int32, (n_maps, width), 0)
        lane = lax.broadcasted_iota(jnp.int32, (n_maps, width), 1)
        qt = jnp.where(lane // head_dim == 2 * (row % n_heads) + row // n_heads, q, 0.0)
        qt_ref[...] = qt.astype(BF16)
        m_ref[...] = jnp.sum(qt * kn_ref[0], axis=-1, keepdims=True)
        l_ref[...] = jnp.ones(l_ref.shape, F32)
        acc_ref[...] = jnp.concatenate([vn_ref[0], vn_ref[0]], axis=0)

    qt = qt_ref[...]
    s = jnp.concatenate(
        [jnp.dot(qt, k_refs[n][...].astype(BF16), preferred_element_type=F32) for n in range(pages)],
        axis=1)
    kpos = p_idx * (pages * page) + lax.broadcasted_iota(jnp.int32, s.shape, 1)
    s = s - slope_ref[...] * (past - kpos).astype(F32)
    m_prev = m_ref[...]
    m_new = jnp.maximum(m_prev, jnp.max(s, axis=-1, keepdims=True))
    alpha = jnp.exp(m_prev - m_new)
    p = jnp.exp(s - m_new)
    l_ref[...] = alpha * l_ref[...] + jnp.sum(p, axis=-1, keepdims=True)
    pstack = jnp.concatenate([p[:, n * page:(n + 1) * page] for n in range(pages)], axis=0).astype(BF16)
    pexp = jnp.dot(pstack, ex_ref[...], preferred_element_type=F32)
    row = lax.broadcasted_iota(jnp.int32, pexp.shape, 0)
    lane = lax.broadcasted_iota(jnp.int32, pexp.shape, 1)
    pexp = jnp.where(lane % n_heads == row % n_heads, pexp, 0.0).astype(BF16)
    pv = jnp.dot(pexp[0:n_maps], v_refs[0][...].astype(BF16), preferred_element_type=F32)
    for n in range(1, pages):
        pv = pv + jnp.dot(pexp[n * n_maps:(n + 1) * n_maps], v_refs[n][...].astype(BF16),
                          preferred_element_type=F32)
    acc_ref[...] = alpha * acc_ref[...] + pv
    m_ref[...] = m_new

    @pl.when(p_idx == pl.num_programs(1) - 1)
    def _finish():
        lam = _diff_lambda(lq1_ref, lk1_ref, lq2_ref, lk2_ref, layer)
        accn = acc_ref[...] / l_ref[...]
        o = accn[0:n_heads] - lam * accn[n_heads:n_maps]
        o_ref[0] = _rmsnorm_rows(o, sg_ref[...]) * (1.0 - _lambda_init(layer))


def _attn_sample(q_bf, k_new, v_new, cache_k, cache_v, a, page_table, slope_rows,
                 lq1, lk1, lq2, lk2, sg, head_dim, v_dim, layer, pages):
    db, width = q_bf.shape
    n_pages = page_table.shape[1]
    n_layers, n_phys, page, n_heads = cache_k.shape[:4]
    assert n_pages % pages == 0 and width == n_heads * 2 * head_dim
    ckt = jnp.transpose(cache_k, (0, 1, 3, 4, 5, 2)).reshape(n_layers, n_phys, width, page)
    cv2 = cache_v.reshape(n_layers, n_phys, page * n_heads, v_dim)
    expand = jnp.asarray(np.arange(page * n_heads)[None, :] // n_heads == np.arange(page)[:, None], dtype=BF16)

    def page_spec(rows, cols, n):
        return pl.BlockSpec((None, None, rows, cols),
                            lambda bi, pi, pt: (a, pt[bi, pi * pages + n], 0, 0))

    rowv = pl.BlockSpec((1, 1, width), lambda bi, pi, pt: (bi, 0, 0))
    small = _const_spec((1, head_dim))
    out = pl.pallas_call(
        functools.partial(_decode_body, pages=pages, page=page, head_dim=head_dim,
                          past=n_pages * page, layer=layer),
        grid_spec=pltpu.PrefetchScalarGridSpec(
            num_scalar_prefetch=1,
            grid=(db, n_pages // pages),
            in_specs=[rowv, rowv, pl.BlockSpec((1, n_heads, v_dim), lambda bi, pi, pt: (bi, 0, 0)),
                      _const_spec((2 * n_heads, 1)), small, small, small, small, _const_spec((1, v_dim)),
                      _const_spec(expand.shape)]
                     + [page_spec(width, page, n) for n in range(pages)]
                     + [page_spec(page * n_heads, v_dim, n) for n in range(pages)],
            out_specs=pl.BlockSpec((1, n_heads, v_dim), lambda bi, pi, pt: (bi, 0, 0)),
            scratch_shapes=[pltpu.VMEM((2 * n_heads, width), BF16),
                            pltpu.VMEM((2 * n_heads, 1), F32),
                            pltpu.VMEM((2 * n_heads, 1), F32),
                            pltpu.VMEM((2 * n_heads, v_dim), F32)]),
        out_shape=jax.ShapeDtypeStruct((db, n_heads, v_dim), F32),
        compiler_params=_cparams("parallel", "arbitrary"),
        name="attn_sample",
    )(page_table, q_bf.reshape(db, 1, width), k_new.reshape(db, 1, width), v_new,
      slope_rows, lq1, lk1, lq2, lk2, sg, expand, *([ckt] * pages), *([cv2] * pages))
    return out.reshape(db, n_heads * v_dim)


def _oproj_body(x_ref, o_ref, w_ref, y_ref):
    y_ref[0] = x_ref[0] + jnp.dot(o_ref[0].astype(BF16), w_ref[...], preferred_element_type=F32)


def _oproj(x, o, w_bf, tm):
    b, tp, d = x.shape
    qw = o.shape[-1]
    row = lambda bi, ti: (bi, ti, 0)
    return pl.pallas_call(
        _oproj_body,
        grid=(b, tp // tm),
        in_specs=[pl.BlockSpec((1, tm, d), row), pl.BlockSpec((1, tm, qw), row), _const_spec(w_bf.shape)],
        out_specs=pl.BlockSpec((1, tm, d), row),
        out_shape=jax.ShapeDtypeStruct((b, tp, d), F32),
        compiler_params=_cparams("parallel", "parallel"),
        name="oproj",
    )(x, o, w_bf)


def _mlp_body(x_ref, g_ref, wu_ref, wd_ref, y_ref, *, ff_chunk):
    x = x_ref[0]
    xn = _rmsnorm_rows(x, g_ref[...]).astype(BF16)
    acc = x
    for c in range(0, wu_ref.shape[1], ff_chunk):
        hcol = jnp.dot(xn, wu_ref[:, c:c + ff_chunk], preferred_element_type=F32)
        hcol = jnp.maximum(hcol, 0.0)
        acc = acc + jnp.dot((hcol * hcol).astype(BF16), wd_ref[c:c + ff_chunk, :],
                            preferred_element_type=F32)
    y_ref[0] = acc


def _mlp(x, g, wu_bf, wd_bf, tm):
    b, tp, d = x.shape
    row = lambda bi, ti: (bi, ti, 0)
    return pl.pallas_call(
        functools.partial(_mlp_body, ff_chunk=min(FF_CHUNK, wu_bf.shape[1])),
        grid=(b, tp // tm),
        in_specs=[pl.BlockSpec((1, tm, d), row), _const_spec((1, d)),
                  _const_spec(wu_bf.shape), _const_spec(wd_bf.shape)],
        out_specs=pl.BlockSpec((1, tm, d), row),
        out_shape=jax.ShapeDtypeStruct((b, tp, d), F32),
        compiler_params=_cparams("parallel", "parallel"),
        name="mlp",
    )(x, g, wu_bf, wd_bf)


def _pool_groups(h, win_fn, inv_cnt_fn, pw_ref, scale):
    gw = pw_ref.shape[-1]
    outs = []
    for gi, w in enumerate(POOL_WINDOWS):
        c0 = gi * gw
        mean = win_fn(w, c0, gw) * inv_cnt_fn(w)
        d = (mean - h[:, c0:c0 + gw]).astype(BF16)
        outs.append(jnp.dot(d, pw_ref[gi], preferred_element_type=F32))
    return jnp.concatenate(outs, axis=-1) * scale


def _pool_prompt_body(x_ref, g_ref, pw_ref, sc_ref, y_ref, tail_ref, hext_ref, *, tm, tail_tile, tail_off):
    t = pl.program_id(1)
    ctx = POOL_CTX + 1

    @pl.when(t == 0)
    def _zero_ctx():
        hext_ref[0:ctx, :] = jnp.zeros((ctx, hext_ref.shape[1]), F32)

    @pl.when(t > 0)
    def _carry_ctx():
        hext_ref[0:ctx, :] = hext_ref[tm:tm + ctx, :]

    x = x_ref[0]
    h = _rmsnorm_rows(x, g_ref[...])
    hext_ref[ctx:ctx + tm, :] = h

    def win(w, c0, gw):
        acc = h[:, c0:c0 + gw]
        for dlt in range(1, w):
            acc = acc + hext_ref[ctx - dlt:ctx - dlt + tm, c0:c0 + gw]
        return acc

    pos = t * tm + lax.broadcasted_iota(jnp.int32, (tm, 1), 0)

    def inv_cnt(w):
        return 1.0 / jnp.minimum(pos + 1, w).astype(F32)

    y_ref[0] = x + _pool_groups(h, win, inv_cnt, pw_ref, sc_ref[...])

    @pl.when(t == tail_tile)
    def _tail():
        tail_ref[0] = hext_ref[ctx + tail_off:ctx + tail_off + ctx, :]


def _pool_prompt(x, t_valid, g, pw_bf, scale, tm):
    b, tp, d = x.shape
    ctx = POOL_CTX + 1
    tail_tile, tail_off = divmod(t_valid - ctx, tm)
    assert tail_off + ctx <= tm
    row = lambda bi, ti: (bi, ti, 0)
    return pl.pallas_call(
        functools.partial(_pool_prompt_body, tm=tm, tail_tile=tail_tile, tail_off=tail_off),
        grid=(b, tp // tm),
        in_specs=[pl.BlockSpec((1, tm, d), row), _const_spec((1, d)),
                  _const_spec(pw_bf.shape), _const_spec((1, d))],
        out_specs=[pl.BlockSpec((1, tm, d), row), pl.BlockSpec((1, ctx, d), lambda bi, ti: (bi, 0, 0))],
        out_shape=[jax.ShapeDtypeStruct((b, tp, d), F32), jax.ShapeDtypeStruct((b, ctx, d), F32)],
        scratch_shapes=[pltpu.VMEM((tm + ctx, d), F32)],
        compiler_params=_cparams("parallel", "arbitrary"),
        name="pool_prompt",
    )(x, g, pw_bf, scale)


def _pool_sample_body(x_ref, st_ref, g_ref, pw_ref, sc_ref, y_ref, h_ref, *, past):
    x = x_ref[...]
    h = _rmsnorm_rows(x, g_ref[...])
    h_ref[...] = h

    def win(w, c0, gw):
        acc = h[:, c0:c0 + gw]
        for dlt in range(1, w):
            acc = acc + st_ref[POOL_CTX - dlt][:, c0:c0 + gw]
        return acc

    def inv_cnt(w):
        return 1.0 / float(min(past + 1, w))

    y_ref[...] = x + _pool_groups(h, win, inv_cnt, pw_ref, sc_ref[...])


def _pool_sample(x, state_t, g, pw_bf, scale, past):
    db, d = x.shape
    return pl.pallas_call(
        functools.partial(_pool_sample_body, past=past),
        grid=(1,),
        in_specs=[_const_spec((db, d)), _const_spec(state_t.shape), _const_spec((1, d)),
                  _const_spec(pw_bf.shape), _const_spec((1, d))],
        out_specs=[pl.BlockSpec((db, d), lambda i: (0, 0))] * 2,
        out_shape=[jax.ShapeDtypeStruct((db, d), F32)] * 2,
        compiler_params=_cparams("arbitrary"),
        name="pool_sample",
    )(x, state_t, g, pw_bf, scale)


def _round_up(n, m):
    return (n + m - 1) // m * m


def kernel(x_prompt, x_sample, cache_k, cache_v, page_table, state_pool, meta_tokens, norm_mix_g, norm_ffn_g, w_qkv, q_norm_g, k_norm_g, lambda_q1, lambda_k1, lambda_q2, lambda_k2, subln_g, w_o, pool_w, pool_scale, w_up, w_down):
    b, seq, d = x_prompt.shape
    db, dec_seq, _ = x_sample.shape
    assert dec_seq == 1
    n_meta = meta_tokens.shape[0]
    n_heads, head_dim = cache_k.shape[3], cache_k.shape[5]
    v_dim = cache_v.shape[4]
    qw = n_heads * 2 * head_dim
    depth = norm_mix_g.shape[0]
    page = cache_k.shape[2]
    past = page_table.shape[1] * page
    t = seq + n_meta

    tm = ROW_TILE if t >= ROW_TILE else _round_up(t, 16)
    tp = _round_up(t, tm)

    meta = jnp.broadcast_to(meta_tokens.astype(x_prompt.dtype)[None], (b, n_meta, d))
    xp = jnp.concatenate([meta, x_prompt, jnp.zeros((b, tp - t, d), x_prompt.dtype)], axis=1)
    xs = x_sample.reshape(1, db, d)

    slopes = jnp.asarray((2.0 ** (-8.0 / n_heads)) ** np.arange(1, n_heads + 1), dtype=F32)
    slope_rows = jnp.tile(slopes, 2).reshape(2 * n_heads, 1)
    blk = np.arange(MXU_DIM_V7X) // head_dim
    bd = jnp.asarray((blk[:, None] == blk[None, :]) / head_dim, dtype=BF16)

    row1 = lambda v: v.reshape(1, -1)
    kp_l, vp_l, ks_l, vs_l, sp_l, ss_l = [], [], [], [], [], []
    for i in range(depth):
        gm = row1(norm_mix_g[i])
        if i % N_MIXERS == 0:
            a = i // N_MIXERS
            w_bf = w_qkv[a].astype(BF16)
            qkv_w = (w_bf[:, 0:qw], w_bf[:, qw:2 * qw].T, w_bf[:, 2 * qw:3 * qw],
                     row1(jnp.tile(q_norm_g[a], qw // head_dim)),
                     jnp.tile(k_norm_g[a], qw // head_dim).reshape(qw, 1), bd, head_dim)
            wo_bf = w_o[a].astype(BF16)
            lam_args = (row1(lambda_q1[a]), row1(lambda_k1[a]), row1(lambda_q2[a]), row1(lambda_k2[a]),
                        row1(subln_g[a]))
            q_bf, kt_f, v_f, kt_bf, v_bf = _qkv(xp, t, gm, *qkv_w, tm)
            o = _attn_prompt(q_bf, kt_bf, v_bf, slopes, *lam_args, head_dim, v_dim, i)
            xp = _oproj(xp, o, wo_bf, tm)
            kp_l.append(jnp.transpose(kt_f.reshape(b, n_heads, 2, head_dim, t), (0, 4, 1, 2, 3)))
            vp_l.append(v_f.reshape(b, t, n_heads, v_dim))

            qs_bf, kst_f, vs_f, _, _ = _qkv(xs, db, gm, *qkv_w, db)
            ks_f = kst_f[0].T
            os_ = _attn_sample(qs_bf[0], ks_f, vs_f.reshape(db, n_heads, v_dim), cache_k, cache_v, a,
                               page_table, slope_rows, *lam_args, head_dim, v_dim, i, DECODE_PAGES_PER_STEP)
            xs = _oproj(xs, os_[None], wo_bf, db)
            ks_l.append(ks_f.reshape(db, 1, n_heads, 2, head_dim))
            vs_l.append(vs_f.reshape(db, 1, n_heads, v_dim))
        else:
            p = i // N_MIXERS
            pw_bf = pool_w[p].astype(BF16)
            sc = row1(pool_scale[p])
            xp, tail = _pool_prompt(xp, t, gm, pw_bf, sc, tm)
            sp_l.append(tail[:, 1:])
            xs2, hs = _pool_sample(xs[0], jnp.swapaxes(state_pool[p], 0, 1), gm, pw_bf, sc, past)
            xs = xs2[None]
            ss_l.append(jnp.concatenate([state_pool[p][:, 1:], hs[:, None]], axis=1))
        gf = row1(norm_ffn_g[i])
        wu_bf = w_up[i].astype(BF16)
        wd_bf = w_down[i].astype(BF16)
        xp = _mlp(xp, gf, wu_bf, wd_bf, tm)
        xs = _mlp(xs, gf, wu_bf, wd_bf, db)

    y_prompt = xp[:, n_meta:t]
    y_sample = xs.reshape(db, 1, d)
    return (y_prompt, y_sample, jnp.stack(kp_l), jnp.stack(vp_l), jnp.stack(ks_l), jnp.stack(vs_l),
            jnp.stack(sp_l), jnp.stack(ss_l))
```

```python
import functools
import math

import numpy as np
import jax
import jax.numpy as jnp
from jax import lax
from jax.experimental import pallas as pl
from jax.experimental.pallas import tpu as pltpu

F32 = jnp.float32
BF16 = jnp.bfloat16

NORM_EPS = 1e-6
POOL_WINDOWS = (2, 4, 8, 16)
POOL_CTX = max(POOL_WINDOWS) - 1
N_MIXERS = 2

LANES = 128
MXU_DIM_V7X = 256
ROW_TILE = 768
ATTN_ROW_CHUNK = 256
FF_CHUNK = 1024
LOG2E = math.log2(math.e)
DECODE_PAGES_PER_STEP = 8
VMEM_LIMIT_BYTES_V7X = 56 * 1024 * 1024
NEG_BIG = -1e30


def _lambda_init(layer):
    return 0.8 - 0.6 * math.exp(-0.3 * layer)


def _cparams(*sem):
    return pltpu.CompilerParams(dimension_semantics=sem, vmem_limit_bytes=VMEM_LIMIT_BYTES_V7X)


def _const_spec(shape):
    nd = len(shape)
    return pl.BlockSpec(shape, lambda *_: (0,) * nd, pipeline_mode=pl.Buffered(1))


def _rmsnorm_rows(x, g):
    ms = jnp.mean(x * x, axis=-1, keepdims=True)
    return x * lax.rsqrt(ms + NORM_EPS) * g


def _diff_lambda(lq1_ref, lk1_ref, lq2_ref, lk2_ref, layer):
    a = jnp.sum(lq1_ref[...] * lk1_ref[...], axis=-1, keepdims=True)
    b = jnp.sum(lq2_ref[...] * lk2_ref[...], axis=-1, keepdims=True)
    return jnp.exp(a) - jnp.exp(b) + _lambda_init(layer)


def _qkv_body(x_ref, g_ref, wq_ref, wkt_ref, wv_ref, qg_ref, kgc_ref, bd_ref,
              q_ref, ktf_ref, vf_ref, ktb_ref, vb_ref, *, head_dim):
    xn = _rmsnorm_rows(x_ref[0], g_ref[...]).astype(BF16)
    tm = xn.shape[0]
    qw = q_ref.shape[-1]

    q = jnp.dot(xn, wq_ref[...], preferred_element_type=F32)
    qq = (q * q).astype(BF16)
    parts = [jnp.dot(qq[:, c:c + MXU_DIM_V7X], bd_ref[...], preferred_element_type=F32)
             for c in range(0, qw, MXU_DIM_V7X)]
    q = q * lax.rsqrt(jnp.concatenate(parts, axis=-1) + NORM_EPS) * qg_ref[...]
    q_ref[0] = (q * (head_dim ** -0.5 * LOG2E)).astype(BF16)

    kt = lax.dot_general(wkt_ref[...], xn, (((1,), (1,)), ((), ())), preferred_element_type=F32)
    kt3 = kt.reshape(qw // head_dim, head_dim, tm)
    ms = jnp.mean(kt3 * kt3, axis=1, keepdims=True)
    kt = (kt3 * lax.rsqrt(ms + NORM_EPS)).reshape(qw, tm) * kgc_ref[...]
    ktf_ref[0] = kt
    ktb_ref[0, :, 0] = kt.astype(BF16).reshape(ktb_ref.shape[1], ktb_ref.shape[3], tm)

    v = jnp.dot(xn, wv_ref[...], preferred_element_type=F32)
    vf_ref[0] = v
    vb_ref[0] = v.astype(BF16)


def _qkv(x, t_valid, g, wq_bf, wkt_bf, wv_bf, qg_t, kg_col, bd, head_dim, tm):
    b, tp, d = x.shape
    qw = wq_bf.shape[1]
    hw = 2 * head_dim
    row = lambda bi, ti: (bi, ti, 0)
    return pl.pallas_call(
        functools.partial(_qkv_body, head_dim=head_dim),
        grid=(b, tp // tm),
        in_specs=[pl.BlockSpec((1, tm, d), row), _const_spec((1, d)),
                  _const_spec(wq_bf.shape), _const_spec(wkt_bf.shape), _const_spec(wv_bf.shape),
                  _const_spec((1, qw)), _const_spec((qw, 1)), _const_spec(bd.shape)],
        out_specs=[pl.BlockSpec((1, tm, qw), row),
                   pl.BlockSpec((1, qw, tm), lambda bi, ti: (bi, 0, ti)),
                   pl.BlockSpec((1, tm, qw), row),
                   pl.BlockSpec((1, qw // hw, 1, hw, tm), lambda bi, ti: (bi, 0, ti, 0, 0)),
                   pl.BlockSpec((1, tm, qw), row)],
        out_shape=[jax.ShapeDtypeStruct((b, tp, qw), BF16),
                   jax.ShapeDtypeStruct((b, qw, t_valid), F32),
                   jax.ShapeDtypeStruct((b, t_valid, qw), F32),
                   jax.ShapeDtypeStruct((b, qw // hw, tp // tm, hw, tm), BF16),
                   jax.ShapeDtypeStruct((b, tp, qw), BF16)],
        compiler_params=_cparams("parallel", "parallel"),
        name="qkv",
    )(x, g, wq_bf, wkt_bf, wv_bf, qg_t, kg_col, bd)


def _attn_body(slopes_ref, q_ref, k_ref, v_ref, lq1_ref, lk1_ref, lq2_ref, lk2_ref, sg_ref, o_ref,
               q2_ref, relb_ref, reli_ref, m_ref, al_ref, p_ref, acc_ref, *, tq, tk, rc, head_dim, layer):
    h = pl.program_id(1)
    i = pl.program_id(2)
    slope = slopes_ref[h]
    vd = v_ref.shape[-1]

    q = q_ref[0]
    lane = lax.broadcasted_iota(jnp.int32, q.shape, 1)
    zero = jnp.zeros_like(q)
    q2_ref[0:tq, :] = jnp.where(lane < head_dim, q, zero)
    q2_ref[tq:2 * tq, :] = jnp.where(lane >= head_dim, q, zero)
    m_ref[...] = jnp.full(m_ref.shape, NEG_BIG, F32)
    acc_ref[...] = jnp.zeros(acc_ref.shape, F32)

    rel = lax.broadcasted_iota(jnp.int32, (rc, tk), 1) - lax.broadcasted_iota(jnp.int32, (rc, tk), 0)
    reli_ref[...] = rel
    relb_ref[...] = rel.astype(F32) * slope

    p_ref[...] = jnp.zeros(p_ref.shape, BF16)
    al_ref[...] = jnp.ones(al_ref.shape, F32)

    def values_with_ones(j):
        vj = v_ref[0, pl.ds(pl.multiple_of(j * tk, tk), tk), :]
        return jnp.concatenate([vj, jnp.ones_like(vj)], axis=1)

    def apply_pending(rows, v1):
        acc_ref[rows, :] = (al_ref[rows, :] * acc_ref[rows, :]
                            + jnp.dot(p_ref[rows, :], v1, preferred_element_type=F32))

    def step(j, masked):
        v1_prev = values_with_ones(jnp.maximum(j - 1, 0))
        kj = k_ref[0, 0, j]
        for r0 in range(0, 2 * tq, rc):
            rows = slice(r0, r0 + rc)
            apply_pending(rows, v1_prev)
            off = j * tk - i * tq - (r0 % tq)
            s = jnp.dot(q2_ref[rows, :], kj, preferred_element_type=F32)
            t = s + relb_ref[...]
            if masked:
                t = jnp.where(reli_ref[...] + off <= 0, t, NEG_BIG)
            c = slope * off.astype(F32)
            tmax = t[:, 0:LANES]
            for c0 in range(LANES, tk, LANES):
                tmax = jnp.maximum(tmax, t[:, c0:c0 + LANES])
            m_prev = m_ref[rows, :]
            m_new = jnp.maximum(m_prev, jnp.max(tmax, axis=-1, keepdims=True) + c)
            al_ref[rows, :] = jnp.exp2(m_prev - m_new)
            p_ref[rows, :] = jnp.exp2(t - (m_new - c)).astype(BF16)
            m_ref[rows, :] = m_new

    n_full = (i * tq + 1) // tk
    n_need = ((i + 1) * tq + tk - 1) // tk

    def full_step(j, carry):
        step(j, False)
        return carry

    def diag_step(j, carry):
        step(j, True)
        return carry

    lax.fori_loop(0, n_full, full_step, 0)
    lax.fori_loop(n_full, n_need, diag_step, 0)
    v1_last = values_with_ones(n_need - 1)
    for r0 in range(0, 2 * tq, rc):
        apply_pending(slice(r0, r0 + rc), v1_last)

    lam = _diff_lambda(lq1_ref, lk1_ref, lq2_ref, lk2_ref, layer)
    o1 = acc_ref[0:tq, 0:vd] / acc_ref[0:tq, vd:2 * vd]
    o2 = acc_ref[tq:2 * tq, 0:vd] / acc_ref[tq:2 * tq, vd:2 * vd]
    o = o1 - lam * o2
    o = _rmsnorm_rows(o, sg_ref[...]) * (1.0 - _lambda_init(layer))
    o_ref[0] = o.astype(o_ref.dtype)


def _attn_prompt(q_bf, kt_bf, v_bf, slopes, lq1, lk1, lq2, lk2, sg, head_dim, v_dim, layer):
    b, tp, qw = q_bf.shape
    n_heads, n_kt, _, ta = kt_bf.shape[1:]
    assert v_dim == 2 * head_dim and n_kt * ta == tp and qw == n_heads * 2 * head_dim
    small = _const_spec((1, head_dim))
    rc = ATTN_ROW_CHUNK if ta % ATTN_ROW_CHUNK == 0 else ta
    return pl.pallas_call(
        functools.partial(_attn_body, tq=ta, tk=ta, rc=rc, head_dim=head_dim, layer=layer),
        grid_spec=pltpu.PrefetchScalarGridSpec(
            num_scalar_prefetch=1,
            grid=(b, n_heads, tp // ta),
            in_specs=[pl.BlockSpec((1, ta, 2 * head_dim), lambda bi, hi, qi, sl: (bi, qi, hi)),
                      pl.BlockSpec((1, 1, n_kt, 2 * head_dim, ta), lambda bi, hi, qi, sl: (bi, hi, 0, 0, 0)),
                      pl.BlockSpec((1, tp, v_dim), lambda bi, hi, qi, sl: (bi, 0, hi)),
                      small, small, small, small, _const_spec((1, v_dim))],
            out_specs=pl.BlockSpec((1, ta, v_dim), lambda bi, hi, qi, sl: (bi, qi, hi)),
            scratch_shapes=[pltpu.VMEM((2 * ta, 2 * head_dim), BF16),
                            pltpu.VMEM((rc, ta), F32),
                            pltpu.VMEM((rc, ta), jnp.int32),
                            pltpu.VMEM((2 * ta, 1), F32),
                            pltpu.VMEM((2 * ta, 1), F32),
                            pltpu.VMEM((2 * ta, ta), BF16),
                            pltpu.VMEM((2 * ta, 2 * v_dim), F32)]),
        out_shape=jax.ShapeDtypeStruct((b, tp, n_heads * v_dim), BF16),
        compiler_params=_cparams("parallel", "parallel", "arbitrary"),
        name="attn_prompt",
    )(slopes, q_bf, kt_bf, v_bf, lq1, lk1, lq2, lk2, sg)


def _decode_body(pt_ref, q_ref, kn_ref, vn_ref, slope_ref, lq1_ref, lk1_ref, lq2_ref, lk2_ref, sg_ref,
                 ex_ref, *refs, pages, page, head_dim, past, layer):
    k_refs = refs[:pages]
    v_refs = refs[pages:2 * pages]
    o_ref = refs[2 * pages]
    qt_ref, m_ref, l_ref, acc_ref = refs[2 * pages + 1:]
    p_idx = pl.program_id(1)
    n_maps, width = qt_ref.shape
    n_heads = n_maps // 2

    @pl.when(p_idx == 0)
    def _init():
        q = q_ref[0].astype(F32)
        row = lax.broadcasted_iota(jnp.int32, (n_maps, width), 0)
        lane = lax.broadcasted_iota(jnp.int32, (n_maps, width), 1)
        qt = jnp.where(lane // head_dim == 2 * (row % n_heads) + row // n_heads, q, 0.0)
        qt_ref[...] = qt.astype(BF16)
        m_ref[...] = jnp.sum(qt * kn_ref[0], axis=-1, keepdims=True)
        l_ref[...] = jnp.ones(l_ref.shape, F32)
        acc_ref[...] = jnp.concatenate([vn_ref[0], vn_ref[0]], axis=0)

    qt = qt_ref[...]
    s = jnp.concatenate(
        [jnp.dot(qt, k_refs[n][...].astype(BF16), preferred_element_type=F32) for n in range(pages)],
        axis=1)
    kpos = p_idx * (pages * page) + lax.broadcasted_iota(jnp.int32, s.shape, 1)
    s = s - slope_ref[...] * (past - kpos).astype(F32)
    m_prev = m_ref[...]
    m_new = jnp.maximum(m_prev, jnp.max(s, axis=-1, keepdims=True))
    alpha = jnp.exp2(m_prev - m_new)
    p = jnp.exp2(s - m_new)
    l_ref[...] = alpha * l_ref[...] + jnp.sum(p, axis=-1, keepdims=True)
    pstack = jnp.concatenate([p[:, n * page:(n + 1) * page] for n in range(pages)], axis=0).astype(BF16)
    pexp = jnp.dot(pstack, ex_ref[...], preferred_element_type=F32)
    row = lax.broadcasted_iota(jnp.int32, pexp.shape, 0)
    lane = lax.broadcasted_iota(jnp.int32, pexp.shape, 1)
    pexp = jnp.where(lane % n_heads == row % n_heads, pexp, 0.0).astype(BF16)
    pv = jnp.dot(pexp[0:n_maps], v_refs[0][...].astype(BF16), preferred_element_type=F32)
    for n in range(1, pages):
        pv = pv + jnp.dot(pexp[n * n_maps:(n + 1) * n_maps], v_refs[n][...].astype(BF16),
                          preferred_element_type=F32)
    acc_ref[...] = alpha * acc_ref[...] + pv
    m_ref[...] = m_new

    @pl.when(p_idx == pl.num_programs(1) - 1)
    def _finish():
        lam = _diff_lambda(lq1_ref, lk1_ref, lq2_ref, lk2_ref, layer)
        accn = acc_ref[...] / l_ref[...]
        o = accn[0:n_heads] - lam * accn[n_heads:n_maps]
        o_ref[0] = _rmsnorm_rows(o, sg_ref[...]) * (1.0 - _lambda_init(layer))


def _attn_sample(q_bf, k_new, v_new, cache_k, cache_v, a, page_table, slope_rows,
                 lq1, lk1, lq2, lk2, sg, head_dim, v_dim, layer, pages):
    db, width = q_bf.shape
    n_pages = page_table.shape[1]
    n_layers, n_phys, page, n_heads = cache_k.shape[:4]
    assert n_pages % pages == 0 and width == n_heads * 2 * head_dim
    ckt = jnp.transpose(cache_k, (0, 1, 3, 4, 5, 2)).reshape(n_layers, n_phys, width, page)
    cv2 = cache_v.reshape(n_layers, n_phys, page * n_heads, v_dim)
    expand = jnp.asarray(np.arange(page * n_heads)[None, :] // n_heads == np.arange(page)[:, None], dtype=BF16)

    def page_spec(rows, cols, n):
        return pl.BlockSpec((None, None, rows, cols),
                            lambda bi, pi, pt: (a, pt[bi, pi * pages + n], 0, 0))

    rowv = pl.BlockSpec((1, 1, width), lambda bi, pi, pt: (bi, 0, 0))
    small = _const_spec((1, head_dim))
    out = pl.pallas_call(
        functools.partial(_decode_body, pages=pages, page=page, head_dim=head_dim,
                          past=n_pages * page, layer=layer),
        grid_spec=pltpu.PrefetchScalarGridSpec(
            num_scalar_prefetch=1,
            grid=(db, n_pages // pages),
            in_specs=[rowv, rowv, pl.BlockSpec((1, n_heads, v_dim), lambda bi, pi, pt: (bi, 0, 0)),
                      _const_spec((2 * n_heads, 1)), small, small, small, small, _const_spec((1, v_dim)),
                      _const_spec(expand.shape)]
                     + [page_spec(width, page, n) for n in range(pages)]
                     + [page_spec(page * n_heads, v_dim, n) for n in range(pages)],
            out_specs=pl.BlockSpec((1, n_heads, v_dim), lambda bi, pi, pt: (bi, 0, 0)),
            scratch_shapes=[pltpu.VMEM((2 * n_heads, width), BF16),
                            pltpu.VMEM((2 * n_heads, 1), F32),
                            pltpu.VMEM((2 * n_heads, 1), F32),
                            pltpu.VMEM((2 * n_heads, v_dim), F32)]),
        out_shape=jax.ShapeDtypeStruct((db, n_heads, v_dim), F32),
        compiler_params=_cparams("parallel", "arbitrary"),
        name="attn_sample",
    )(page_table, q_bf.reshape(db, 1, width), k_new.reshape(db, 1, width), v_new,
      slope_rows, lq1, lk1, lq2, lk2, sg, expand, *([ckt] * pages), *([cv2] * pages))
    return out.reshape(db, n_heads * v_dim)


def _oproj_body(x_ref, o_ref, w_ref, y_ref):
    y_ref[0] = x_ref[0] + jnp.dot(o_ref[0].astype(BF16), w_ref[...], preferred_element_type=F32)


def _oproj(x, o, w_bf, tm):
    b, tp, d = x.shape
    qw = o.shape[-1]
    row = lambda bi, ti: (bi, ti, 0)
    return pl.pallas_call(
        _oproj_body,
        grid=(b, tp // tm),
        in_specs=[pl.BlockSpec((1, tm, d), row), pl.BlockSpec((1, tm, qw), row), _const_spec(w_bf.shape)],
        out_specs=pl.BlockSpec((1, tm, d), row),
        out_shape=jax.ShapeDtypeStruct((b, tp, d), F32),
        compiler_params=_cparams("parallel", "parallel"),
        name="oproj",
    )(x, o, w_bf)


def _mlp_body(x_ref, g_ref, wu_ref, wd_ref, y_ref, *, ff_chunk):
    x = x_ref[0]
    xn = _rmsnorm_rows(x, g_ref[...]).astype(BF16)
    acc = x
    for c in range(0, wu_ref.shape[1], ff_chunk):
        hcol = jnp.dot(xn, wu_ref[:, c:c + ff_chunk], preferred_element_type=F32)
        hcol = jnp.maximum(hcol, 0.0)
        acc = acc + jnp.dot((hcol * hcol).astype(BF16), wd_ref[c:c + ff_chunk, :],
                            preferred_element_type=F32)
    y_ref[0] = acc


def _mlp(x, g, wu_bf, wd_bf, tm):
    b, tp, d = x.shape
    row = lambda bi, ti: (bi, ti, 0)
    return pl.pallas_call(
        functools.partial(_mlp_body, ff_chunk=min(FF_CHUNK, wu_bf.shape[1])),
        grid=(b, tp // tm),
        in_specs=[pl.BlockSpec((1, tm, d), row), _const_spec((1, d)),
                  _const_spec(wu_bf.shape), _const_spec(wd_bf.shape)],
        out_specs=pl.BlockSpec((1, tm, d), row),
        out_shape=jax.ShapeDtypeStruct((b, tp, d), F32),
        compiler_params=_cparams("parallel", "parallel"),
        name="mlp",
    )(x, g, wu_bf, wd_bf)


def _pool_groups(h, win_fn, inv_cnt_fn, pw_ref, scale):
    gw = pw_ref.shape[-1]
    outs = []
    for gi, w in enumerate(POOL_WINDOWS):
        c0 = gi * gw
        mean = win_fn(w, c0, gw) * inv_cnt_fn(w)
        d = (mean - h[:, c0:c0 + gw]).astype(BF16)
        outs.append(jnp.dot(d, pw_ref[gi], preferred_element_type=F32))
    return jnp.concatenate(outs, axis=-1) * scale


def _pool_prompt_body(x_ref, g_ref, pw_ref, sc_ref, y_ref, tail_ref, hext_ref, *, tm, tail_tile, tail_off):
    t = pl.program_id(1)
    ctx = POOL_CTX + 1

    @pl.when(t == 0)
    def _zero_ctx():
        hext_ref[0:ctx, :] = jnp.zeros((ctx, hext_ref.shape[1]), F32)

    @pl.when(t > 0)
    def _carry_ctx():
        hext_ref[0:ctx, :] = hext_ref[tm:tm + ctx, :]

    x = x_ref[0]
    h = _rmsnorm_rows(x, g_ref[...])
    hext_ref[ctx:ctx + tm, :] = h

    def win(w, c0, gw):
        acc = h[:, c0:c0 + gw]
        for dlt in range(1, w):
            acc = acc + hext_ref[ctx - dlt:ctx - dlt + tm, c0:c0 + gw]
        return acc

    pos = t * tm + lax.broadcasted_iota(jnp.int32, (tm, 1), 0)

    def inv_cnt(w):
        return 1.0 / jnp.minimum(pos + 1, w).astype(F32)

    y_ref[0] = x + _pool_groups(h, win, inv_cnt, pw_ref, sc_ref[...])

    @pl.when(t == tail_tile)
    def _tail():
        tail_ref[0] = hext_ref[ctx + tail_off:ctx + tail_off + ctx, :]


def _pool_prompt(x, t_valid, g, pw_bf, scale, tm):
    b, tp, d = x.shape
    ctx = POOL_CTX + 1
    tail_tile, tail_off = divmod(t_valid - ctx, tm)
    assert tail_off + ctx <= tm
    row = lambda bi, ti: (bi, ti, 0)
    return pl.pallas_call(
        functools.partial(_pool_prompt_body, tm=tm, tail_tile=tail_tile, tail_off=tail_off),
        grid=(b, tp // tm),
        in_specs=[pl.BlockSpec((1, tm, d), row), _const_spec((1, d)),
                  _const_spec(pw_bf.shape), _const_spec((1, d))],
        out_specs=[pl.BlockSpec((1, tm, d), row), pl.BlockSpec((1, ctx, d), lambda bi, ti: (bi, 0, 0))],
        out_shape=[jax.ShapeDtypeStruct((b, tp, d), F32), jax.ShapeDtypeStruct((b, ctx, d), F32)],
        scratch_shapes=[pltpu.VMEM((tm + ctx, d), F32)],
        compiler_params=_cparams("parallel", "arbitrary"),
        name="pool_prompt",
    )(x, g, pw_bf, scale)


def _pool_sample_body(x_ref, st_ref, g_ref, pw_ref, sc_ref, y_ref, h_ref, *, past):
    x = x_ref[...]
    h = _rmsnorm_rows(x, g_ref[...])
    h_ref[...] = h

    def win(w, c0, gw):
        acc = h[:, c0:c0 + gw]
        for dlt in range(1, w):
            acc = acc + st_ref[POOL_CTX - dlt][:, c0:c0 + gw]
        return acc

    def inv_cnt(w):
        return 1.0 / float(min(past + 1, w))

    y_ref[...] = x + _pool_groups(h, win, inv_cnt, pw_ref, sc_ref[...])


def _pool_sample(x, state_t, g, pw_bf, scale, past):
    db, d = x.shape
    return pl.pallas_call(
        functools.partial(_pool_sample_body, past=past),
        grid=(1,),
        in_specs=[_const_spec((db, d)), _const_spec(state_t.shape), _const_spec((1, d)),
                  _const_spec(pw_bf.shape), _const_spec((1, d))],
        out_specs=[pl.BlockSpec((db, d), lambda i: (0, 0))] * 2,
        out_shape=[jax.ShapeDtypeStruct((db, d), F32)] * 2,
        compiler_params=_cparams("arbitrary"),
        name="pool_sample",
    )(x, state_t, g, pw_bf, scale)


def _round_up(n, m):
    return (n + m - 1) // m * m


def kernel(x_prompt, x_sample, cache_k, cache_v, page_table, state_pool, meta_tokens, norm_mix_g, norm_ffn_g, w_qkv, q_norm_g, k_norm_g, lambda_q1, lambda_k1, lambda_q2, lambda_k2, subln_g, w_o, pool_w, pool_scale, w_up, w_down):
    b, seq, d = x_prompt.shape
    db, dec_seq, _ = x_sample.shape
    assert dec_seq == 1
    n_meta = meta_tokens.shape[0]
    n_heads, head_dim = cache_k.shape[3], cache_k.shape[5]
    v_dim = cache_v.shape[4]
    qw = n_heads * 2 * head_dim
    depth = norm_mix_g.shape[0]
    page = cache_k.shape[2]
    past = page_table.shape[1] * page
    t = seq + n_meta

    tm = ROW_TILE if t >= ROW_TILE else _round_up(t, 16)
    tp = _round_up(t, tm)

    meta = jnp.broadcast_to(meta_tokens.astype(x_prompt.dtype)[None], (b, n_meta, d))
    xp = jnp.concatenate([meta, x_prompt, jnp.zeros((b, tp - t, d), x_prompt.dtype)], axis=1)
    xs = x_sample.reshape(1, db, d)

    slopes = jnp.asarray(LOG2E * (2.0 ** (-8.0 / n_heads)) ** np.arange(1, n_heads + 1), dtype=F32)
    slope_rows = jnp.tile(slopes, 2).reshape(2 * n_heads, 1)
    blk = np.arange(MXU_DIM_V7X) // head_dim
    bd = jnp.asarray((blk[:, None] == blk[None, :]) / head_dim, dtype=BF16)

    row1 = lambda v: v.reshape(1, -1)
    kp_l, vp_l, ks_l, vs_l, sp_l, ss_l = [], [], [], [], [], []
    for i in range(depth):
        gm = row1(norm_mix_g[i])
        if i % N_MIXERS == 0:
            a = i // N_MIXERS
            w_bf = w_qkv[a].astype(BF16)
            qkv_w = (w_bf[:, 0:qw], w_bf[:, qw:2 * qw].T, w_bf[:, 2 * qw:3 * qw],
                     row1(jnp.tile(q_norm_g[a], qw // head_dim)),
                     jnp.tile(k_norm_g[a], qw // head_dim).reshape(qw, 1), bd, head_dim)
            wo_bf = w_o[a].astype(BF16)
            lam_args = (row1(lambda_q1[a]), row1(lambda_k1[a]), row1(lambda_q2[a]), row1(lambda_k2[a]),
                        row1(subln_g[a]))
            q_bf, kt_f, v_f, kt_bf, v_bf = _qkv(xp, t, gm, *qkv_w, tm)
            o = _attn_prompt(q_bf, kt_bf, v_bf, slopes, *lam_args, head_dim, v_dim, i)
            xp = _oproj(xp, o, wo_bf, tm)
            kp_l.append(jnp.transpose(kt_f.reshape(b, n_heads, 2, head_dim, t), (0, 4, 1, 2, 3)))
            vp_l.append(v_f.reshape(b, t, n_heads, v_dim))

            qs_bf, kst_f, vs_f, _, _ = _qkv(xs, db, gm, *qkv_w, db)
            ks_f = kst_f[0].T
            os_ = _attn_sample(qs_bf[0], ks_f, vs_f.reshape(db, n_heads, v_dim), cache_k, cache_v, a,
                               page_table, slope_rows, *lam_args, head_dim, v_dim, i, DECODE_PAGES_PER_STEP)
            xs = _oproj(xs, os_[None], wo_bf, db)
            ks_l.append(ks_f.reshape(db, 1, n_heads, 2, head_dim))
            vs_l.append(vs_f.reshape(db, 1, n_heads, v_dim))
        else:
            p = i // N_MIXERS
            pw_bf = pool_w[p].astype(BF16)
            sc = row1(pool_scale[p])
            xp, tail = _pool_prompt(xp, t, gm, pw_bf, sc, tm)
            sp_l.append(tail[:, 1:])
            xs2, hs = _pool_sample(xs[0], jnp.swapaxes(state_pool[p], 0, 1), gm, pw_bf, sc, past)
            xs = xs2[None]
            ss_l.append(jnp.concatenate([state_pool[p][:, 1:], hs[:, None]], axis=1))
        gf = row1(norm_ffn_g[i])
        wu_bf = w_up[i].astype(BF16)
        wd_bf = w_down[i].astype(BF16)
        xp = _mlp(xp, gf, wu_bf, wd_bf, tm)
        xs = _mlp(xs, gf, wu_bf, wd_bf, db)

    y_prompt = xp[:, n_meta:t]
    y_sample = xs.reshape(db, 1, d)
    return (y_prompt, y_sample, jnp.stack(kp_l), jnp.stack(vp_l), jnp.stack(ks_l), jnp.stack(vs_l),
            jnp.stack(sp_l), jnp.stack(ss_l))
```

```python
import functools
import math

import numpy as np
import jax
import jax.numpy as jnp
from jax import lax
from jax.experimental import pallas as pl
from jax.experimental.pallas import tpu as pltpu

F32 = jnp.float32
BF16 = jnp.bfloat16

NORM_EPS = 1e-6
POOL_WINDOWS = (2, 4, 8, 16)
POOL_CTX = max(POOL_WINDOWS) - 1
N_MIXERS = 2

LANES = 128
MXU_DIM_V7X = 256
ROW_TILE = 768
ATTN_ROW_CHUNK = 256
SOFTMAX_ROW_CHUNK = 32
FF_CHUNK = 1024
LOG2E = math.log2(math.e)
DECODE_PAGES_PER_STEP = 8
VMEM_LIMIT_BYTES_V7X = 56 * 1024 * 1024
NEG_BIG = -1e30


def _lambda_init(layer):
    return 0.8 - 0.6 * math.exp(-0.3 * layer)


def _cparams(*sem):
    return pltpu.CompilerParams(dimension_semantics=sem, vmem_limit_bytes=VMEM_LIMIT_BYTES_V7X)


def _const_spec(shape):
    nd = len(shape)
    return pl.BlockSpec(shape, lambda *_: (0,) * nd, pipeline_mode=pl.Buffered(1))


def _rmsnorm_rows(x, g):
    ms = jnp.mean(x * x, axis=-1, keepdims=True)
    return x * lax.rsqrt(ms + NORM_EPS) * g


def _diff_lambda(lq1_ref, lk1_ref, lq2_ref, lk2_ref, layer):
    a = jnp.sum(lq1_ref[...] * lk1_ref[...], axis=-1, keepdims=True)
    b = jnp.sum(lq2_ref[...] * lk2_ref[...], axis=-1, keepdims=True)
    return jnp.exp(a) - jnp.exp(b) + _lambda_init(layer)


def _qkv_body(x_ref, g_ref, wq_ref, wkt_ref, wv_ref, qg_ref, kgc_ref, bd_ref,
              q_ref, ktf_ref, vf_ref, ktb_ref, vb_ref, *, head_dim):
    xn = _rmsnorm_rows(x_ref[0], g_ref[...]).astype(BF16)
    tm = xn.shape[0]
    qw = q_ref.shape[-1]

    q = jnp.dot(xn, wq_ref[...], preferred_element_type=F32)
    qq = (q * q).astype(BF16)
    parts = [jnp.dot(qq[:, c:c + MXU_DIM_V7X], bd_ref[...], preferred_element_type=F32)
             for c in range(0, qw, MXU_DIM_V7X)]
    q = q * lax.rsqrt(jnp.concatenate(parts, axis=-1) + NORM_EPS) * qg_ref[...]
    q_ref[0] = (q * (head_dim ** -0.5 * LOG2E)).astype(BF16)

    kt = lax.dot_general(wkt_ref[...], xn, (((1,), (1,)), ((), ())), preferred_element_type=F32)
    kt3 = kt.reshape(qw // head_dim, head_dim, tm)
    ms = jnp.mean(kt3 * kt3, axis=1, keepdims=True)
    kt = (kt3 * lax.rsqrt(ms + NORM_EPS)).reshape(qw, tm) * kgc_ref[...]
    ktf_ref[0] = kt
    ktb_ref[0, :, 0] = kt.astype(BF16).reshape(ktb_ref.shape[1], ktb_ref.shape[3], tm)

    v = jnp.dot(xn, wv_ref[...], preferred_element_type=F32)
    vf_ref[0] = v
    vb_ref[0] = v.astype(BF16)


def _qkv(x, t_valid, g, wq_bf, wkt_bf, wv_bf, qg_t, kg_col, bd, head_dim, tm):
    b, tp, d = x.shape
    qw = wq_bf.shape[1]
    hw = 2 * head_dim
    row = lambda bi, ti: (bi, ti, 0)
    return pl.pallas_call(
        functools.partial(_qkv_body, head_dim=head_dim),
        grid=(b, tp // tm),
        in_specs=[pl.BlockSpec((1, tm, d), row), _const_spec((1, d)),
                  _const_spec(wq_bf.shape), _const_spec(wkt_bf.shape), _const_spec(wv_bf.shape),
                  _const_spec((1, qw)), _const_spec((qw, 1)), _const_spec(bd.shape)],
        out_specs=[pl.BlockSpec((1, tm, qw), row),
                   pl.BlockSpec((1, qw, tm), lambda bi, ti: (bi, 0, ti)),
                   pl.BlockSpec((1, tm, qw), row),
                   pl.BlockSpec((1, qw // hw, 1, hw, tm), lambda bi, ti: (bi, 0, ti, 0, 0)),
                   pl.BlockSpec((1, tm, qw), row)],
        out_shape=[jax.ShapeDtypeStruct((b, tp, qw), BF16),
                   jax.ShapeDtypeStruct((b, qw, t_valid), F32),
                   jax.ShapeDtypeStruct((b, t_valid, qw), F32),
                   jax.ShapeDtypeStruct((b, qw // hw, tp // tm, hw, tm), BF16),
                   jax.ShapeDtypeStruct((b, tp, qw), BF16)],
        compiler_params=_cparams("parallel", "parallel"),
        name="qkv",
    )(x, g, wq_bf, wkt_bf, wv_bf, qg_t, kg_col, bd)


def _attn_body(slopes_ref, pi_ref, pj_ref, plast_ref,
               q_ref, k_ref, v_ref, lq1_ref, lk1_ref, lq2_ref, lk2_ref, sg_ref, o_ref,
               bias_ref, s0_ref, s1_ref, tm0_ref, tm1_ref, p0_ref, p1_ref, al0_ref, al1_ref, m_ref, acc_ref,
               *, t, rc, rb, n_pairs, head_dim, layer):
    slope = slopes_ref[pl.program_id(1)]
    vd = v_ref.shape[-1]
    chunks = range(0, 2 * t, rc)
    s_refs, p_refs, al_refs = (s0_ref, s1_ref), (p0_ref, p1_ref), (al0_ref, al1_ref)
    tm_refs = (tm0_ref, tm1_ref)

    rel = lax.broadcasted_iota(jnp.int32, (rc, t), 1) - lax.broadcasted_iota(jnp.int32, (rc, t), 0)
    relb = rel.astype(F32) * slope
    bias_ref[0] = relb
    for k in range(t // rc):
        bias_ref[1 + k] = jnp.where(rel - k * rc <= 0, relb, NEG_BIG)

    m_ref[...] = jnp.full(m_ref.shape, NEG_BIG, F32)
    acc_ref[...] = jnp.zeros(acc_ref.shape, F32)
    p1_ref[...] = jnp.zeros(p1_ref.shape, BF16)
    al1_ref[...] = jnp.ones(al1_ref.shape, F32)

    def scores(n, buf):
        i, j = pi_ref[n], pj_ref[n]
        kj = k_ref[0, 0, j]
        for r0 in chunks:
            qc = q_ref[0, pl.ds(pl.multiple_of(i * t + r0 % t, rc), rc), :]
            lane = lax.broadcasted_iota(jnp.int32, qc.shape, 1)
            keep = lane < head_dim if r0 < t else lane >= head_dim
            slot = jnp.where(i == j, 1 + (r0 % t) // rc, 0)
            tt = bias_ref[slot] + jnp.dot(jnp.where(keep, qc, jnp.zeros_like(qc)), kj,
                                          preferred_element_type=F32)
            s_refs[buf][r0:r0 + rc, :] = tt
            tmax = tt[:, 0:LANES]
            for c0 in range(LANES, t, LANES):
                tmax = jnp.maximum(tmax, tt[:, c0:c0 + LANES])
            tm_refs[buf][r0:r0 + rc, :] = tmax

    def softmax(n, buf):
        i, j = pi_ref[n], pj_ref[n]
        for r0 in range(0, 2 * t, rb):
            rows = slice(r0, r0 + rb)
            c = slope * ((j - i) * t - (r0 // rc * rc) % t).astype(F32)
            m_prev = jnp.where(j == 0, NEG_BIG, m_ref[rows, :])
            m_new = jnp.maximum(m_prev, jnp.max(tm_refs[buf][rows, :], axis=-1, keepdims=True) + c)
            al_refs[buf][rows, :] = jnp.exp2(m_prev - m_new)
            p_refs[buf][rows, :] = jnp.exp2(s_refs[buf][rows, :] - (m_new - c)).astype(BF16)
            m_ref[rows, :] = m_new

    def values(n, buf):
        j = pj_ref[n]
        vj = v_ref[0, pl.ds(pl.multiple_of(j * t, t), t), :]
        v1 = jnp.concatenate([vj, jnp.ones_like(vj)], axis=1)
        for r0 in chunks:
            rows = slice(r0, r0 + rc)
            acc_ref[rows, :] = (al_refs[buf][rows, :] * acc_ref[rows, :]
                                + jnp.dot(p_refs[buf][rows, :], v1, preferred_element_type=F32))

    def finish_tile(n):
        i = pi_ref[n]
        lam = _diff_lambda(lq1_ref, lk1_ref, lq2_ref, lk2_ref, layer)
        o1 = acc_ref[0:t, 0:vd] / acc_ref[0:t, vd:2 * vd]
        o2 = acc_ref[t:2 * t, 0:vd] / acc_ref[t:2 * t, vd:2 * vd]
        o = _rmsnorm_rows(o1 - lam * o2, sg_ref[...]) * (1.0 - _lambda_init(layer))
        o_ref[0, pl.ds(pl.multiple_of(i * t, t), t), :] = o.astype(o_ref.dtype)

    def step(n, buf):
        scores(jnp.minimum(n + 1, n_pairs - 1), 1 - buf)
        softmax(n, buf)
        prev = jnp.maximum(n - 1, 0)
        values(prev, 1 - buf)

        @pl.when(jnp.logical_and(n >= 1, plast_ref[prev] == 1))
        def _():
            finish_tile(prev)

    scores(0, 0)

    def two_steps(k, carry):
        step(2 * k, 0)
        step(2 * k + 1, 1)
        return carry

    lax.fori_loop(0, n_pairs // 2, two_steps, 0)
    if n_pairs % 2:
        step(n_pairs - 1, 0)
    values(n_pairs - 1, (n_pairs - 1) % 2)
    finish_tile(n_pairs - 1)


def _attn_prompt(q_bf, kt_bf, v_bf, slopes, lq1, lk1, lq2, lk2, sg, head_dim, v_dim, layer):
    b, tp, qw = q_bf.shape
    n_heads, n_t, _, t = kt_bf.shape[1:]
    assert v_dim == 2 * head_dim and n_t * t == tp and qw == n_heads * 2 * head_dim
    rc = ATTN_ROW_CHUNK if t % ATTN_ROW_CHUNK == 0 else t
    rb = SOFTMAX_ROW_CHUNK if t % SOFTMAX_ROW_CHUNK == 0 else t
    pairs = [(i, j) for i in range(n_t) for j in range(i + 1)]
    pi = jnp.asarray([p[0] for p in pairs], jnp.int32)
    pj = jnp.asarray([p[1] for p in pairs], jnp.int32)
    plast = jnp.asarray([int(p[0] == p[1]) for p in pairs], jnp.int32)
    hw = 2 * head_dim
    small = _const_spec((1, head_dim))
    per_head = lambda shape, imap: pl.BlockSpec(shape, imap, pipeline_mode=pl.Buffered(1))
    return pl.pallas_call(
        functools.partial(_attn_body, t=t, rc=rc, rb=rb, n_pairs=len(pairs), head_dim=head_dim, layer=layer),
        grid_spec=pltpu.PrefetchScalarGridSpec(
            num_scalar_prefetch=4,
            grid=(b, n_heads),
            in_specs=[per_head((1, tp, hw), lambda bi, hi, *_: (bi, 0, hi)),
                      per_head((1, 1, n_t, hw, t), lambda bi, hi, *_: (bi, hi, 0, 0, 0)),
                      per_head((1, tp, v_dim), lambda bi, hi, *_: (bi, 0, hi)),
                      small, small, small, small, _const_spec((1, v_dim))],
            out_specs=pl.BlockSpec((1, tp, v_dim), lambda bi, hi, *_: (bi, 0, hi)),
            scratch_shapes=[pltpu.VMEM((1 + t // rc, rc, t), F32),
                            pltpu.VMEM((2 * t, t), F32), pltpu.VMEM((2 * t, t), F32),
                            pltpu.VMEM((2 * t, LANES), F32), pltpu.VMEM((2 * t, LANES), F32),
                            pltpu.VMEM((2 * t, t), BF16), pltpu.VMEM((2 * t, t), BF16),
                            pltpu.VMEM((2 * t, 1), F32), pltpu.VMEM((2 * t, 1), F32),
                            pltpu.VMEM((2 * t, 1), F32),
                            pltpu.VMEM((2 * t, 2 * v_dim), F32)]),
        out_shape=jax.ShapeDtypeStruct((b, tp, n_heads * v_dim), BF16),
        compiler_params=_cparams("parallel", "arbitrary"),
        name="attn_prompt",
    )(slopes, pi, pj, plast, q_bf, kt_bf, v_bf, lq1, lk1, lq2, lk2, sg)


def _decode_body(pt_ref, q_ref, kn_ref, vn_ref, slope_ref, lq1_ref, lk1_ref, lq2_ref, lk2_ref, sg_ref,
                 ex_ref, *refs, pages, page, head_dim, past, layer):
    k_refs = refs[:pages]
    v_refs = refs[pages:2 * pages]
    o_ref = refs[2 * pages]
    qt_ref, m_ref, l_ref, acc_ref = refs[2 * pages + 1:]
    p_idx = pl.program_id(1)
    n_maps, width = qt_ref.shape
    n_heads = n_maps // 2

    @pl.when(p_idx == 0)
    def _init():
        q = q_ref[0].astype(F32)
        row = lax.broadcasted_iota(jnp.int32, (n_maps, width), 0)
        lane = lax.broadcasted_iota(jnp.int32, (n_maps, width), 1)
        qt = jnp.where(lane // head_dim == 2 * (row % n_heads) + row // n_heads, q, 0.0)
        qt_ref[...] = qt.astype(BF16)
        m_ref[...] = jnp.sum(qt * kn_ref[0], axis=-1, keepdims=True)
        l_ref[...] = jnp.ones(l_ref.shape, F32)
        acc_ref[...] = jnp.concatenate([vn_ref[0], vn_ref[0]], axis=0)

    qt = qt_ref[...]
    s = jnp.concatenate(
        [jnp.dot(qt, k_refs[n][...].astype(BF16), preferred_element_type=F32) for n in range(pages)],
        axis=1)
    kpos = p_idx * (pages * page) + lax.broadcasted_iota(jnp.int32, s.shape, 1)
    s = s - slope_ref[...] * (past - kpos).astype(F32)
    m_prev = m_ref[...]
    m_new = jnp.maximum(m_prev, jnp.max(s, axis=-1, keepdims=True))
    alpha = jnp.exp2(m_prev - m_new)
    p = jnp.exp2(s - m_new)
    l_ref[...] = alpha * l_ref[...] + jnp.sum(p, axis=-1, keepdims=True)
    pstack = jnp.concatenate([p[:, n * page:(n + 1) * page] for n in range(pages)], axis=0).astype(BF16)
    pexp = jnp.dot(pstack, ex_ref[...], preferred_element_type=F32)
    row = lax.broadcasted_iota(jnp.int32, pexp.shape, 0)
    lane = lax.broadcasted_iota(jnp.int32, pexp.shape, 1)
    pexp = jnp.where(lane % n_heads == row % n_heads, pexp, 0.0).astype(BF16)
    pv = jnp.dot(pexp[0:n_maps], v_refs[0][...].astype(BF16), preferred_element_type=F32)
    for n in range(1, pages):
        pv = pv + jnp.dot(pexp[n * n_maps:(n + 1) * n_maps], v_refs[n][...].astype(BF16),
                          preferred_element_type=F32)
    acc_ref[...] = alpha * acc_ref[...] + pv
    m_ref[...] = m_new

    @pl.when(p_idx == pl.num_programs(1) - 1)
    def _finish():
        lam = _diff_lambda(lq1_ref, lk1_ref, lq2_ref, lk2_ref, layer)
        accn = acc_ref[...] / l_ref[...]
        o = accn[0:n_heads] - lam * accn[n_heads:n_maps]
        o_ref[0] = _rmsnorm_rows(o, sg_ref[...]) * (1.0 - _lambda_init(layer))


def _attn_sample(q_bf, k_new, v_new, cache_k, cache_v, a, page_table, slope_rows,
                 lq1, lk1, lq2, lk2, sg, head_dim, v_dim, layer, pages):
    db, width = q_bf.shape
    n_pages = page_table.shape[1]
    n_layers, n_phys, page, n_heads = cache_k.shape[:4]
    assert n_pages % pages == 0 and width == n_heads * 2 * head_dim
    ckt = jnp.transpose(cache_k, (0, 1, 3, 4, 5, 2)).reshape(n_layers, n_phys, width, page)
    cv2 = cache_v.reshape(n_layers, n_phys, page * n_heads, v_dim)
    expand = jnp.asarray(np.arange(page * n_heads)[None, :] // n_heads == np.arange(page)[:, None], dtype=BF16)

    def page_spec(rows, cols, n):
        return pl.BlockSpec((None, None, rows, cols),
                            lambda bi, pi, pt: (a, pt[bi, pi * pages + n], 0, 0))

    rowv = pl.BlockSpec((1, 1, width), lambda bi, pi, pt: (bi, 0, 0))
    small = _const_spec((1, head_dim))
    out = pl.pallas_call(
        functools.partial(_decode_body, pages=pages, page=page, head_dim=head_dim,
                          past=n_pages * page, layer=layer),
        grid_spec=pltpu.PrefetchScalarGridSpec(
            num_scalar_prefetch=1,
            grid=(db, n_pages // pages),
            in_specs=[rowv, rowv, pl.BlockSpec((1, n_heads, v_dim), lambda bi, pi, pt: (bi, 0, 0)),
                      _const_spec((2 * n_heads, 1)), small, small, small, small, _const_spec((1, v_dim)),
                      _const_spec(expand.shape)]
                     + [page_spec(width, page, n) for n in range(pages)]
                     + [page_spec(page * n_heads, v_dim, n) for n in range(pages)],
            out_specs=pl.BlockSpec((1, n_heads, v_dim), lambda bi, pi, pt: (bi, 0, 0)),
            scratch_shapes=[pltpu.VMEM((2 * n_heads, width), BF16),
                            pltpu.VMEM((2 * n_heads, 1), F32),
                            pltpu.VMEM((2 * n_heads, 1), F32),
                            pltpu.VMEM((2 * n_heads, v_dim), F32)]),
        out_shape=jax.ShapeDtypeStruct((db, n_heads, v_dim), F32),
        compiler_params=_cparams("parallel", "arbitrary"),
        name="attn_sample",
    )(page_table, q_bf.reshape(db, 1, width), k_new.reshape(db, 1, width), v_new,
      slope_rows, lq1, lk1, lq2, lk2, sg, expand, *([ckt] * pages), *([cv2] * pages))
    return out.reshape(db, n_heads * v_dim)


def _mlp_body(*refs, ff_chunk, shift, has_mix):
    refs = list(refs)

    def take_rows():
        a = refs.pop(0)[0]
        if shift:
            a = jnp.concatenate([a[shift:], refs.pop(0)[0]], axis=0)
        return a

    x = take_rows()
    if has_mix:
        o = take_rows()
        x = x + jnp.dot(o.astype(BF16), refs.pop(0)[...], preferred_element_type=F32)
    g_ref, wu_ref, wd_ref, y_ref = refs
    xn = _rmsnorm_rows(x, g_ref[...]).astype(BF16)
    acc = x
    for c in range(0, wu_ref.shape[1], ff_chunk):
        hcol = jnp.dot(xn, wu_ref[:, c:c + ff_chunk], preferred_element_type=F32)
        hcol = jnp.maximum(hcol, 0.0)
        acc = acc + jnp.dot((hcol * hcol).astype(BF16), wd_ref[c:c + ff_chunk, :],
                            preferred_element_type=F32)
    y_ref[0] = acc


def _mlp(x, g, wu_bf, wd_bf, tm, mix=None, shift=0, out_rows=None):
    b, tp, d = x.shape
    out_rows = tp if out_rows is None else out_rows
    assert shift == 0 or (shift % 8 == 0 and tm % shift == 0 and tp % shift == 0)
    row = lambda bi, ti: (bi, ti, 0)
    nxt = lambda bi, ti: (bi, jnp.minimum((ti + 1) * (tm // shift), tp // shift - 1), 0)

    def row_specs(width):
        specs = [pl.BlockSpec((1, tm, width), row)]
        return specs + [pl.BlockSpec((1, shift, width), nxt)] if shift else specs

    operands = [x, x] if shift else [x]
    in_specs = row_specs(d)
    if mix is not None:
        o, wo_bf = mix
        operands += [o, o, wo_bf] if shift else [o, wo_bf]
        in_specs += row_specs(o.shape[-1]) + [_const_spec(wo_bf.shape)]
    return pl.pallas_call(
        functools.partial(_mlp_body, ff_chunk=min(FF_CHUNK, wu_bf.shape[1]), shift=shift,
                          has_mix=mix is not None),
        grid=(b, pl.cdiv(out_rows, tm)),
        in_specs=in_specs + [_const_spec((1, d)), _const_spec(wu_bf.shape), _const_spec(wd_bf.shape)],
        out_specs=pl.BlockSpec((1, tm, d), row),
        out_shape=jax.ShapeDtypeStruct((b, out_rows, d), F32),
        compiler_params=_cparams("parallel", "parallel"),
        name="mlp",
    )(*operands, g, wu_bf, wd_bf)


def _pool_groups(h, win_fn, inv_cnt_fn, pw_ref, scale):
    gw = pw_ref.shape[-1]
    outs = []
    for gi, w in enumerate(POOL_WINDOWS):
        c0 = gi * gw
        mean = win_fn(w, c0, gw) * inv_cnt_fn(w)
        d = (mean - h[:, c0:c0 + gw]).astype(BF16)
        outs.append(jnp.dot(d, pw_ref[gi], preferred_element_type=F32))
    return jnp.concatenate(outs, axis=-1) * scale


def _pool_prompt_body(x_ref, g_ref, pw_ref, sc_ref, y_ref, tail_ref, hext_ref, *, tm, tail_tile, tail_off):
    t = pl.program_id(1)
    ctx = POOL_CTX + 1

    @pl.when(t == 0)
    def _zero_ctx():
        hext_ref[0:ctx, :] = jnp.zeros((ctx, hext_ref.shape[1]), F32)

    @pl.when(t > 0)
    def _carry_ctx():
        hext_ref[0:ctx, :] = hext_ref[tm:tm + ctx, :]

    x = x_ref[0]
    h = _rmsnorm_rows(x, g_ref[...])
    hext_ref[ctx:ctx + tm, :] = h

    def win(w, c0, gw):
        acc = h[:, c0:c0 + gw]
        for dlt in range(1, w):
            acc = acc + hext_ref[ctx - dlt:ctx - dlt + tm, c0:c0 + gw]
        return acc

    pos = t * tm + lax.broadcasted_iota(jnp.int32, (tm, 1), 0)

    def inv_cnt(w):
        return 1.0 / jnp.minimum(pos + 1, w).astype(F32)

    y_ref[0] = x + _pool_groups(h, win, inv_cnt, pw_ref, sc_ref[...])

    @pl.when(t == tail_tile)
    def _tail():
        tail_ref[0] = hext_ref[ctx + tail_off:ctx + tail_off + ctx, :]


def _pool_prompt(x, t_valid, g, pw_bf, scale, tm):
    b, tp, d = x.shape
    ctx = POOL_CTX + 1
    tail_tile, tail_off = divmod(t_valid - ctx, tm)
    assert tail_off + ctx <= tm
    row = lambda bi, ti: (bi, ti, 0)
    return pl.pallas_call(
        functools.partial(_pool_prompt_body, tm=tm, tail_tile=tail_tile, tail_off=tail_off),
        grid=(b, tp // tm),
        in_specs=[pl.BlockSpec((1, tm, d), row), _const_spec((1, d)),
                  _const_spec(pw_bf.shape), _const_spec((1, d))],
        out_specs=[pl.BlockSpec((1, tm, d), row), pl.BlockSpec((1, ctx, d), lambda bi, ti: (bi, 0, 0))],
        out_shape=[jax.ShapeDtypeStruct((b, tp, d), F32), jax.ShapeDtypeStruct((b, ctx, d), F32)],
        scratch_shapes=[pltpu.VMEM((tm + ctx, d), F32)],
        compiler_params=_cparams("parallel", "arbitrary"),
        name="pool_prompt",
    )(x, g, pw_bf, scale)


def _pool_sample_body(x_ref, st_ref, g_ref, pw_ref, sc_ref, y_ref, h_ref, *, past):
    x = x_ref[...]
    h = _rmsnorm_rows(x, g_ref[...])
    h_ref[...] = h

    def win(w, c0, gw):
        acc = h[:, c0:c0 + gw]
        for dlt in range(1, w):
            acc = acc + st_ref[POOL_CTX - dlt][:, c0:c0 + gw]
        return acc

    def inv_cnt(w):
        return 1.0 / float(min(past + 1, w))

    y_ref[...] = x + _pool_groups(h, win, inv_cnt, pw_ref, sc_ref[...])


def _pool_sample(x, state_t, g, pw_bf, scale, past):
    db, d = x.shape
    return pl.pallas_call(
        functools.partial(_pool_sample_body, past=past),
        grid=(1,),
        in_specs=[_const_spec((db, d)), _const_spec(state_t.shape), _const_spec((1, d)),
                  _const_spec(pw_bf.shape), _const_spec((1, d))],
        out_specs=[pl.BlockSpec((db, d), lambda i: (0, 0))] * 2,
        out_shape=[jax.ShapeDtypeStruct((db, d), F32)] * 2,
        compiler_params=_cparams("arbitrary"),
        name="pool_sample",
    )(x, state_t, g, pw_bf, scale)


def _round_up(n, m):
    return (n + m - 1) // m * m


def kernel(x_prompt, x_sample, cache_k, cache_v, page_table, state_pool, meta_tokens, norm_mix_g, norm_ffn_g, w_qkv, q_norm_g, k_norm_g, lambda_q1, lambda_k1, lambda_q2, lambda_k2, subln_g, w_o, pool_w, pool_scale, w_up, w_down):
    b, seq, d = x_prompt.shape
    db, dec_seq, _ = x_sample.shape
    assert dec_seq == 1
    n_meta = meta_tokens.shape[0]
    n_heads, head_dim = cache_k.shape[3], cache_k.shape[5]
    v_dim = cache_v.shape[4]
    qw = n_heads * 2 * head_dim
    depth = norm_mix_g.shape[0]
    page = cache_k.shape[2]
    past = page_table.shape[1] * page
    t = seq + n_meta

    tm = ROW_TILE if t >= ROW_TILE else _round_up(t, 16)
    tp = _round_up(t, tm)

    meta = jnp.broadcast_to(meta_tokens.astype(x_prompt.dtype)[None], (b, n_meta, d))
    xp = jnp.concatenate([meta, x_prompt, jnp.zeros((b, tp - t, d), x_prompt.dtype)], axis=1)
    xs = x_sample.reshape(1, db, d)

    slopes = jnp.asarray(LOG2E * (2.0 ** (-8.0 / n_heads)) ** np.arange(1, n_heads + 1), dtype=F32)
    slope_rows = jnp.tile(slopes, 2).reshape(2 * n_heads, 1)
    blk = np.arange(MXU_DIM_V7X) // head_dim
    bd = jnp.asarray((blk[:, None] == blk[None, :]) / head_dim, dtype=BF16)

    row1 = lambda v: v.reshape(1, -1)
    kp_l, vp_l, ks_l, vs_l, sp_l, ss_l = [], [], [], [], [], []
    for i in range(depth):
        gm = row1(norm_mix_g[i])
        if i % N_MIXERS == 0:
            a = i // N_MIXERS
            w_bf = w_qkv[a].astype(BF16)
            qkv_w = (w_bf[:, 0:qw], w_bf[:, qw:2 * qw].T, w_bf[:, 2 * qw:3 * qw],
                     row1(jnp.tile(q_norm_g[a], qw // head_dim)),
                     jnp.tile(k_norm_g[a], qw // head_dim).reshape(qw, 1), bd, head_dim)
            wo_bf = w_o[a].astype(BF16)
            lam_args = (row1(lambda_q1[a]), row1(lambda_k1[a]), row1(lambda_q2[a]), row1(lambda_k2[a]),
                        row1(subln_g[a]))
            q_bf, kt_f, v_f, kt_bf, v_bf = _qkv(xp, t, gm, *qkv_w, tm)
            mix_p = (_attn_prompt(q_bf, kt_bf, v_bf, slopes, *lam_args, head_dim, v_dim, i), wo_bf)
            kp_l.append(jnp.transpose(kt_f.reshape(b, n_heads, 2, head_dim, t), (0, 4, 1, 2, 3)))
            vp_l.append(v_f.reshape(b, t, n_heads, v_dim))

            qs_bf, kst_f, vs_f, _, _ = _qkv(xs, db, gm, *qkv_w, db)
            ks_f = kst_f[0].T
            os_ = _attn_sample(qs_bf[0], ks_f, vs_f.reshape(db, n_heads, v_dim), cache_k, cache_v, a,
                               page_table, slope_rows, *lam_args, head_dim, v_dim, i, DECODE_PAGES_PER_STEP)
            mix_s = (os_[None], wo_bf)
            ks_l.append(ks_f.reshape(db, 1, n_heads, 2, head_dim))
            vs_l.append(vs_f.reshape(db, 1, n_heads, v_dim))
        else:
            p = i // N_MIXERS
            pw_bf = pool_w[p].astype(BF16)
            sc = row1(pool_scale[p])
            xp, tail = _pool_prompt(xp, t, gm, pw_bf, sc, tm)
            sp_l.append(tail[:, 1:])
            xs2, hs = _pool_sample(xs[0], jnp.swapaxes(state_pool[p], 0, 1), gm, pw_bf, sc, past)
            xs = xs2[None]
            mix_p = mix_s = None
            ss_l.append(jnp.concatenate([state_pool[p][:, 1:], hs[:, None]], axis=1))
        gf = row1(norm_ffn_g[i])
        wu_bf = w_up[i].astype(BF16)
        wd_bf = w_down[i].astype(BF16)
        if i == depth - 1:
            y_prompt = _mlp(xp, gf, wu_bf, wd_bf, tm, mix_p, shift=n_meta, out_rows=seq)
        else:
            xp = _mlp(xp, gf, wu_bf, wd_bf, tm, mix_p)
        xs = _mlp(xs, gf, wu_bf, wd_bf, db, mix_s)

    y_sample = xs.reshape(db, 1, d)
    return (y_prompt, y_sample, jnp.stack(kp_l), jnp.stack(vp_l), jnp.stack(ks_l), jnp.stack(vs_l),
            jnp.stack(sp_l), jnp.stack(ss_l))
```

```python
import functools
import math

import numpy as np
import jax
import jax.numpy as jnp
from jax import lax
from jax.experimental import pallas as pl
from jax.experimental.pallas import tpu as pltpu

F32 = jnp.float32
BF16 = jnp.bfloat16

NORM_EPS = 1e-6
POOL_WINDOWS = (2, 4, 8, 16)
POOL_CTX = max(POOL_WINDOWS) - 1
N_MIXERS = 2

LANES = 128
MXU_DIM_V7X = 256
ROW_TILE = 768
ATTN_ROW_CHUNK = 256
SOFTMAX_ROW_CHUNK = 16
FF_CHUNK = 1024
LOG2E = math.log2(math.e)
DECODE_PAGES_PER_STEP = 16
VMEM_LIMIT_BYTES_V7X = 56 * 1024 * 1024
NEG_BIG = -1e30


def _lambda_init(layer):
    return 0.8 - 0.6 * math.exp(-0.3 * layer)


def _cparams(*sem):
    return pltpu.CompilerParams(dimension_semantics=sem, vmem_limit_bytes=VMEM_LIMIT_BYTES_V7X)


def _const_spec(shape):
    nd = len(shape)
    return pl.BlockSpec(shape, lambda *_: (0,) * nd, pipeline_mode=pl.Buffered(1))


def _rmsnorm_rows(x, g):
    ms = jnp.mean(x * x, axis=-1, keepdims=True)
    return x * lax.rsqrt(ms + NORM_EPS) * g


def _diff_lambda(lq1_ref, lk1_ref, lq2_ref, lk2_ref, layer):
    a = jnp.sum(lq1_ref[...] * lk1_ref[...], axis=-1, keepdims=True)
    b = jnp.sum(lq2_ref[...] * lk2_ref[...], axis=-1, keepdims=True)
    return jnp.exp(a) - jnp.exp(b) + _lambda_init(layer)


def _qkv_body(x_ref, g_ref, wq_ref, wkt_ref, wv_ref, qg_ref, kgc_ref, bd_ref,
              q_ref, ktf_ref, vf_ref, ktb_ref, vb_ref, *, head_dim):
    xn = _rmsnorm_rows(x_ref[0], g_ref[...]).astype(BF16)
    tm = xn.shape[0]
    qw = q_ref.shape[-1]

    q = jnp.dot(xn, wq_ref[...], preferred_element_type=F32)
    qq = (q * q).astype(BF16)
    parts = [jnp.dot(qq[:, c:c + MXU_DIM_V7X], bd_ref[...], preferred_element_type=F32)
             for c in range(0, qw, MXU_DIM_V7X)]
    q = q * lax.rsqrt(jnp.concatenate(parts, axis=-1) + NORM_EPS) * qg_ref[...]
    q_ref[0] = (q * (head_dim ** -0.5 * LOG2E)).astype(BF16)

    kt = lax.dot_general(wkt_ref[...], xn, (((1,), (1,)), ((), ())), preferred_element_type=F32)
    kt3 = kt.reshape(qw // head_dim, head_dim, tm)
    ms = jnp.mean(kt3 * kt3, axis=1, keepdims=True)
    kt = (kt3 * lax.rsqrt(ms + NORM_EPS)).reshape(qw, tm) * kgc_ref[...]
    ktf_ref[0] = kt
    ktb_ref[0, :, 0] = kt.astype(BF16).reshape(ktb_ref.shape[1], ktb_ref.shape[3], tm)

    v = jnp.dot(xn, wv_ref[...], preferred_element_type=F32)
    vf_ref[0] = v
    vb_ref[0] = v.astype(BF16)


def _qkv(x, t_valid, g, wq_bf, wkt_bf, wv_bf, qg_t, kg_col, bd, head_dim, tm):
    b, tp, d = x.shape
    qw = wq_bf.shape[1]
    hw = 2 * head_dim
    row = lambda bi, ti: (bi, ti, 0)
    return pl.pallas_call(
        functools.partial(_qkv_body, head_dim=head_dim),
        grid=(b, tp // tm),
        in_specs=[pl.BlockSpec((1, tm, d), row), _const_spec((1, d)),
                  _const_spec(wq_bf.shape), _const_spec(wkt_bf.shape), _const_spec(wv_bf.shape),
                  _const_spec((1, qw)), _const_spec((qw, 1)), _const_spec(bd.shape)],
        out_specs=[pl.BlockSpec((1, tm, qw), row),
                   pl.BlockSpec((1, qw, tm), lambda bi, ti: (bi, 0, ti)),
                   pl.BlockSpec((1, tm, qw), row),
                   pl.BlockSpec((1, qw // hw, 1, hw, tm), lambda bi, ti: (bi, 0, ti, 0, 0)),
                   pl.BlockSpec((1, tm, qw), row)],
        out_shape=[jax.ShapeDtypeStruct((b, tp, qw), BF16),
                   jax.ShapeDtypeStruct((b, qw, t_valid), F32),
                   jax.ShapeDtypeStruct((b, t_valid, qw), F32),
                   jax.ShapeDtypeStruct((b, qw // hw, tp // tm, hw, tm), BF16),
                   jax.ShapeDtypeStruct((b, tp, qw), BF16)],
        compiler_params=_cparams("parallel", "parallel"),
        name="qkv",
    )(x, g, wq_bf, wkt_bf, wv_bf, qg_t, kg_col, bd)


def _attn_body(slopes_ref, pi_ref, pj_ref, plast_ref,
               q_ref, k_ref, v_ref, lq1_ref, lk1_ref, lq2_ref, lk2_ref, sg_ref, o_ref,
               bias_ref, s0_ref, s1_ref, tm0_ref, tm1_ref, p0_ref, p1_ref, al0_ref, al1_ref, m_ref, acc_ref,
               *, t, rc, rb, n_pairs, head_dim, layer):
    slope = slopes_ref[pl.program_id(1)]
    vd = v_ref.shape[-1]
    chunks = range(0, 2 * t, rc)
    s_refs, p_refs, al_refs = (s0_ref, s1_ref), (p0_ref, p1_ref), (al0_ref, al1_ref)
    tm_refs = (tm0_ref, tm1_ref)

    rel = lax.broadcasted_iota(jnp.int32, (rc, t), 1) - lax.broadcasted_iota(jnp.int32, (rc, t), 0)
    relb = rel.astype(F32) * slope
    bias_ref[0] = relb
    for k in range(t // rc):
        bias_ref[1 + k] = jnp.where(rel - k * rc <= 0, relb, NEG_BIG)

    m_ref[...] = jnp.full(m_ref.shape, NEG_BIG, F32)
    acc_ref[...] = jnp.zeros(acc_ref.shape, F32)
    p1_ref[...] = jnp.zeros(p1_ref.shape, BF16)
    al1_ref[...] = jnp.ones(al1_ref.shape, F32)

    def scores(n, buf):
        i, j = pi_ref[n], pj_ref[n]
        kj = k_ref[0, 0, j]
        for r0 in chunks:
            qc = q_ref[0, pl.ds(pl.multiple_of(i * t + r0 % t, rc), rc), :]
            lane = lax.broadcasted_iota(jnp.int32, qc.shape, 1)
            keep = lane < head_dim if r0 < t else lane >= head_dim
            slot = jnp.where(i == j, 1 + (r0 % t) // rc, 0)
            tt = bias_ref[slot] + jnp.dot(jnp.where(keep, qc, jnp.zeros_like(qc)), kj,
                                          preferred_element_type=F32)
            s_refs[buf][r0:r0 + rc, :] = tt
            tmax = tt[:, 0:LANES]
            for c0 in range(LANES, t, LANES):
                tmax = jnp.maximum(tmax, tt[:, c0:c0 + LANES])
            tm_refs[buf][r0:r0 + rc, :] = tmax

    def softmax(n, buf):
        i, j = pi_ref[n], pj_ref[n]
        for r0 in range(0, 2 * t, rb):
            rows = slice(r0, r0 + rb)
            c = slope * ((j - i) * t - (r0 // rc * rc) % t).astype(F32)
            m_prev = jnp.where(j == 0, NEG_BIG, m_ref[rows, :])
            m_new = jnp.maximum(m_prev, jnp.max(tm_refs[buf][rows, :], axis=-1, keepdims=True) + c)
            al_refs[buf][rows, :] = jnp.exp2(m_prev - m_new)
            p_refs[buf][rows, :] = jnp.exp2(s_refs[buf][rows, :] - (m_new - c)).astype(BF16)
            m_ref[rows, :] = m_new

    def values(n, buf):
        j = pj_ref[n]
        vj = v_ref[0, pl.ds(pl.multiple_of(j * t, t), t), :]
        v1 = jnp.concatenate([vj, jnp.ones_like(vj)], axis=1)
        for r0 in chunks:
            rows = slice(r0, r0 + rc)
            acc_ref[rows, :] = (al_refs[buf][rows, :] * acc_ref[rows, :]
                                + jnp.dot(p_refs[buf][rows, :], v1, preferred_element_type=F32))

    def finish_tile(n):
        i = pi_ref[n]
        lam = _diff_lambda(lq1_ref, lk1_ref, lq2_ref, lk2_ref, layer)
        o1 = acc_ref[0:t, 0:vd] / acc_ref[0:t, vd:2 * vd]
        o2 = acc_ref[t:2 * t, 0:vd] / acc_ref[t:2 * t, vd:2 * vd]
        o = _rmsnorm_rows(o1 - lam * o2, sg_ref[...]) * (1.0 - _lambda_init(layer))
        o_ref[0, pl.ds(pl.multiple_of(i * t, t), t), :] = o.astype(o_ref.dtype)

    def step(n, buf):
        scores(jnp.minimum(n + 1, n_pairs - 1), 1 - buf)
        softmax(n, buf)
        prev = jnp.maximum(n - 1, 0)
        values(prev, 1 - buf)

        @pl.when(jnp.logical_and(n >= 1, plast_ref[prev] == 1))
        def _():
            finish_tile(prev)

    scores(0, 0)

    def two_steps(k, carry):
        step(2 * k, 0)
        step(2 * k + 1, 1)
        return carry

    lax.fori_loop(0, n_pairs // 2, two_steps, 0)
    if n_pairs % 2:
        step(n_pairs - 1, 0)
    values(n_pairs - 1, (n_pairs - 1) % 2)
    finish_tile(n_pairs - 1)


def _attn_prompt(q_bf, kt_bf, v_bf, slopes, lq1, lk1, lq2, lk2, sg, head_dim, v_dim, layer):
    b, tp, qw = q_bf.shape
    n_heads, n_t, _, t = kt_bf.shape[1:]
    assert v_dim == 2 * head_dim and n_t * t == tp and qw == n_heads * 2 * head_dim
    rc = ATTN_ROW_CHUNK if t % ATTN_ROW_CHUNK == 0 else t
    rb = SOFTMAX_ROW_CHUNK if t % SOFTMAX_ROW_CHUNK == 0 else t
    pairs = [(i, j) for i in range(n_t) for j in range(i + 1)]
    pi = jnp.asarray([p[0] for p in pairs], jnp.int32)
    pj = jnp.asarray([p[1] for p in pairs], jnp.int32)
    plast = jnp.asarray([int(p[0] == p[1]) for p in pairs], jnp.int32)
    hw = 2 * head_dim
    small = _const_spec((1, head_dim))
    per_head = pl.BlockSpec
    return pl.pallas_call(
        functools.partial(_attn_body, t=t, rc=rc, rb=rb, n_pairs=len(pairs), head_dim=head_dim, layer=layer),
        grid_spec=pltpu.PrefetchScalarGridSpec(
            num_scalar_prefetch=4,
            grid=(b, n_heads),
            in_specs=[per_head((1, tp, hw), lambda bi, hi, *_: (bi, 0, hi)),
                      per_head((1, 1, n_t, hw, t), lambda bi, hi, *_: (bi, hi, 0, 0, 0)),
                      per_head((1, tp, v_dim), lambda bi, hi, *_: (bi, 0, hi)),
                      small, small, small, small, _const_spec((1, v_dim))],
            out_specs=pl.BlockSpec((1, tp, v_dim), lambda bi, hi, *_: (bi, 0, hi)),
            scratch_shapes=[pltpu.VMEM((1 + t // rc, rc, t), F32),
                            pltpu.VMEM((2 * t, t), F32), pltpu.VMEM((2 * t, t), F32),
                            pltpu.VMEM((2 * t, LANES), F32), pltpu.VMEM((2 * t, LANES), F32),
                            pltpu.VMEM((2 * t, t), BF16), pltpu.VMEM((2 * t, t), BF16),
                            pltpu.VMEM((2 * t, 1), F32), pltpu.VMEM((2 * t, 1), F32),
                            pltpu.VMEM((2 * t, 1), F32),
                            pltpu.VMEM((2 * t, 2 * v_dim), F32)]),
        out_shape=jax.ShapeDtypeStruct((b, tp, n_heads * v_dim), BF16),
        compiler_params=_cparams("parallel", "arbitrary"),
        name="attn_prompt",
    )(slopes, pi, pj, plast, q_bf, kt_bf, v_bf, lq1, lk1, lq2, lk2, sg)


def _decode_body(pt_ref, q_ref, kn_ref, vn_ref, slope_ref, lq1_ref, lk1_ref, lq2_ref, lk2_ref, sg_ref,
                 ex_ref, *refs, pages, page, head_dim, past, layer):
    k_refs = refs[:pages]
    v_refs = refs[pages:2 * pages]
    o_ref = refs[2 * pages]
    qt_ref, m_ref, l_ref, acc_ref = refs[2 * pages + 1:]
    p_idx = pl.program_id(1)
    n_maps, width = qt_ref.shape
    n_heads = n_maps // 2

    @pl.when(p_idx == 0)
    def _init():
        q = q_ref[0].astype(F32)
        row = lax.broadcasted_iota(jnp.int32, (n_maps, width), 0)
        lane = lax.broadcasted_iota(jnp.int32, (n_maps, width), 1)
        qt = jnp.where(lane // head_dim == 2 * (row % n_heads) + row // n_heads, q, 0.0)
        qt_ref[...] = qt.astype(BF16)
        m_ref[...] = jnp.sum(qt * kn_ref[0], axis=-1, keepdims=True)
        l_ref[...] = jnp.ones(l_ref.shape, F32)
        acc_ref[...] = jnp.concatenate([vn_ref[0], vn_ref[0]], axis=0)

    qt = qt_ref[...]
    s = jnp.concatenate(
        [jnp.dot(qt, k_refs[n][...].astype(BF16), preferred_element_type=F32) for n in range(pages)],
        axis=1)
    kpos = p_idx * (pages * page) + lax.broadcasted_iota(jnp.int32, s.shape, 1)
    s = s - slope_ref[...] * (past - kpos).astype(F32)
    m_prev = m_ref[...]
    m_new = jnp.maximum(m_prev, jnp.max(s, axis=-1, keepdims=True))
    alpha = jnp.exp2(m_prev - m_new)
    p = jnp.exp2(s - m_new)
    l_ref[...] = alpha * l_ref[...] + jnp.sum(p, axis=-1, keepdims=True)
    pstack = jnp.concatenate([p[:, n * page:(n + 1) * page] for n in range(pages)], axis=0).astype(BF16)
    pexp = jnp.dot(pstack, ex_ref[...], preferred_element_type=F32)
    row = lax.broadcasted_iota(jnp.int32, pexp.shape, 0)
    lane = lax.broadcasted_iota(jnp.int32, pexp.shape, 1)
    pexp = jnp.where(lane % n_heads == row % n_heads, pexp, 0.0).astype(BF16)
    pv = jnp.dot(pexp[0:n_maps], v_refs[0][...].astype(BF16), preferred_element_type=F32)
    for n in range(1, pages):
        pv = pv + jnp.dot(pexp[n * n_maps:(n + 1) * n_maps], v_refs[n][...].astype(BF16),
                          preferred_element_type=F32)
    acc_ref[...] = alpha * acc_ref[...] + pv
    m_ref[...] = m_new

    @pl.when(p_idx == pl.num_programs(1) - 1)
    def _finish():
        lam = _diff_lambda(lq1_ref, lk1_ref, lq2_ref, lk2_ref, layer)
        accn = acc_ref[...] / l_ref[...]
        o = accn[0:n_heads] - lam * accn[n_heads:n_maps]
        o_ref[0] = _rmsnorm_rows(o, sg_ref[...]) * (1.0 - _lambda_init(layer))


def _attn_sample(q_bf, k_new, v_new, cache_k, cache_v, a, page_table, slope_rows,
                 lq1, lk1, lq2, lk2, sg, head_dim, v_dim, layer, pages):
    db, width = q_bf.shape
    n_pages = page_table.shape[1]
    n_layers, n_phys, page, n_heads = cache_k.shape[:4]
    assert n_pages % pages == 0 and width == n_heads * 2 * head_dim
    ckt = jnp.transpose(cache_k, (0, 1, 3, 4, 5, 2)).reshape(n_layers, n_phys, width, page)
    cv2 = cache_v.reshape(n_layers, n_phys, page * n_heads, v_dim)
    expand = jnp.asarray(np.arange(page * n_heads)[None, :] // n_heads == np.arange(page)[:, None], dtype=BF16)

    def page_spec(rows, cols, n):
        return pl.BlockSpec((None, None, rows, cols),
                            lambda bi, pi, pt: (a, pt[bi, pi * pages + n], 0, 0))

    rowv = pl.BlockSpec((1, 1, width), lambda bi, pi, pt: (bi, 0, 0))
    small = _const_spec((1, head_dim))
    out = pl.pallas_call(
        functools.partial(_decode_body, pages=pages, page=page, head_dim=head_dim,
                          past=n_pages * page, layer=layer),
        grid_spec=pltpu.PrefetchScalarGridSpec(
            num_scalar_prefetch=1,
            grid=(db, n_pages // pages),
            in_specs=[rowv, rowv, pl.BlockSpec((1, n_heads, v_dim), lambda bi, pi, pt: (bi, 0, 0)),
                      _const_spec((2 * n_heads, 1)), small, small, small, small, _const_spec((1, v_dim)),
                      _const_spec(expand.shape)]
                     + [page_spec(width, page, n) for n in range(pages)]
                     + [page_spec(page * n_heads, v_dim, n) for n in range(pages)],
            out_specs=pl.BlockSpec((1, n_heads, v_dim), lambda bi, pi, pt: (bi, 0, 0)),
            scratch_shapes=[pltpu.VMEM((2 * n_heads, width), BF16),
                            pltpu.VMEM((2 * n_heads, 1), F32),
                            pltpu.VMEM((2 * n_heads, 1), F32),
                            pltpu.VMEM((2 * n_heads, v_dim), F32)]),
        out_shape=jax.ShapeDtypeStruct((db, n_heads, v_dim), F32),
        compiler_params=_cparams("parallel", "arbitrary"),
        name="attn_sample",
    )(page_table, q_bf.reshape(db, 1, width), k_new.reshape(db, 1, width), v_new,
      slope_rows, lq1, lk1, lq2, lk2, sg, expand, *([ckt] * pages), *([cv2] * pages))
    return out.reshape(db, n_heads * v_dim)


def _mlp_body(*refs, ff_chunk, shift, has_mix):
    refs = list(refs)

    def take_rows():
        a = refs.pop(0)[0]
        if shift:
            a = jnp.concatenate([a[shift:], refs.pop(0)[0]], axis=0)
        return a

    x = take_rows()
    if has_mix:
        o = take_rows()
        x = x + jnp.dot(o.astype(BF16), refs.pop(0)[...], preferred_element_type=F32)
    g_ref, wu_ref, wd_ref, y_ref = refs
    xn = _rmsnorm_rows(x, g_ref[...]).astype(BF16)
    acc = x
    for c in range(0, wu_ref.shape[1], ff_chunk):
        hcol = jnp.dot(xn, wu_ref[:, c:c + ff_chunk], preferred_element_type=F32)
        hcol = jnp.maximum(hcol, 0.0)
        acc = acc + jnp.dot((hcol * hcol).astype(BF16), wd_ref[c:c + ff_chunk, :],
                            preferred_element_type=F32)
    y_ref[0] = acc


def _mlp(x, g, wu_bf, wd_bf, tm, mix=None, shift=0, out_rows=None):
    b, tp, d = x.shape
    out_rows = tp if out_rows is None else out_rows
    assert shift == 0 or (shift % 8 == 0 and tm % shift == 0 and tp % shift == 0)
    row = lambda bi, ti: (bi, ti, 0)
    nxt = lambda bi, ti: (bi, jnp.minimum((ti + 1) * (tm // shift), tp // shift - 1), 0)

    def row_specs(width):
        specs = [pl.BlockSpec((1, tm, width), row)]
        return specs + [pl.BlockSpec((1, shift, width), nxt)] if shift else specs

    operands = [x, x] if shift else [x]
    in_specs = row_specs(d)
    if mix is not None:
        o, wo_bf = mix
        operands += [o, o, wo_bf] if shift else [o, wo_bf]
        in_specs += row_specs(o.shape[-1]) + [_const_spec(wo_bf.shape)]
    return pl.pallas_call(
        functools.partial(_mlp_body, ff_chunk=min(FF_CHUNK, wu_bf.shape[1]), shift=shift,
                          has_mix=mix is not None),
        grid=(b, pl.cdiv(out_rows, tm)),
        in_specs=in_specs + [_const_spec((1, d)), _const_spec(wu_bf.shape), _const_spec(wd_bf.shape)],
        out_specs=pl.BlockSpec((1, tm, d), row),
        out_shape=jax.ShapeDtypeStruct((b, out_rows, d), F32),
        compiler_params=_cparams("parallel", "parallel"),
        name="mlp",
    )(*operands, g, wu_bf, wd_bf)


def _pool_groups(h, win_fn, inv_cnt_fn, pw_ref, scale):
    gw = pw_ref.shape[-1]
    outs = []
    for gi, w in enumerate(POOL_WINDOWS):
        c0 = gi * gw
        mean = win_fn(w, c0, gw) * inv_cnt_fn(w)
        d = (mean - h[:, c0:c0 + gw]).astype(BF16)
        outs.append(jnp.dot(d, pw_ref[gi], preferred_element_type=F32))
    return jnp.concatenate(outs, axis=-1) * scale


def _pool_prompt_body(x_ref, g_ref, pw_ref, sc_ref, y_ref, tail_ref, *ext_refs, tm, tail_tile, tail_off):
    t = pl.program_id(1)
    ctx = POOL_CTX + 1
    hext_ref = ext_refs[0]
    d = hext_ref.shape[1]
    gw = pw_ref.shape[-1]

    @pl.when(t == 0)
    def _zero_ctx():
        for ref in ext_refs:
            ref[0:ctx, :] = jnp.zeros((ctx, ref.shape[1]), F32)

    @pl.when(t > 0)
    def _carry_ctx():
        for ref in ext_refs:
            ref[0:ctx, :] = ref[tm:tm + ctx, :]

    x = x_ref[0]
    h = _rmsnorm_rows(x, g_ref[...])
    hext_ref[ctx:ctx + tm, :] = h

    sums = {}
    cur = h
    for k, ref in enumerate(ext_refs):
        w = 2 ** k
        cur = cur + ref[ctx - w:ctx - w + tm, :]
        sums[2 * w] = cur[:, 0:gw]
        cur = cur[:, gw:]
        if k + 1 < len(ext_refs):
            ext_refs[k + 1][ctx:ctx + tm, :] = cur

    def win(w, c0, gw_):
        return sums[w]

    pos = t * tm + lax.broadcasted_iota(jnp.int32, (tm, 1), 0)

    def inv_cnt(w):
        return 1.0 / jnp.minimum(pos + 1, w).astype(F32)

    y_ref[0] = x + _pool_groups(h, win, inv_cnt, pw_ref, sc_ref[...])

    @pl.when(t == tail_tile)
    def _tail():
        tail_ref[0] = hext_ref[ctx + tail_off:ctx + tail_off + ctx, :]


def _pool_prompt(x, t_valid, g, pw_bf, scale, tm):
    b, tp, d = x.shape
    ctx = POOL_CTX + 1
    tail_tile, tail_off = divmod(t_valid - ctx, tm)
    assert tail_off + ctx <= tm
    assert POOL_WINDOWS == tuple(2 ** (k + 1) for k in range(len(POOL_WINDOWS)))
    gw = pw_bf.shape[-1]
    row = lambda bi, ti: (bi, ti, 0)
    return pl.pallas_call(
        functools.partial(_pool_prompt_body, tm=tm, tail_tile=tail_tile, tail_off=tail_off),
        grid=(b, tp // tm),
        in_specs=[pl.BlockSpec((1, tm, d), row), _const_spec((1, d)),
                  _const_spec(pw_bf.shape), _const_spec((1, d))],
        out_specs=[pl.BlockSpec((1, tm, d), row), pl.BlockSpec((1, ctx, d), lambda bi, ti: (bi, 0, 0))],
        out_shape=[jax.ShapeDtypeStruct((b, tp, d), F32), jax.ShapeDtypeStruct((b, ctx, d), F32)],
        scratch_shapes=[pltpu.VMEM((tm + ctx, d - k * gw), F32) for k in range(len(POOL_WINDOWS))],
        compiler_params=_cparams("parallel", "arbitrary"),
        name="pool_prompt",
    )(x, g, pw_bf, scale)


def _pool_sample_body(x_ref, st_ref, g_ref, pw_ref, sc_ref, y_ref, h_ref, *, past):
    x = x_ref[...]
    h = _rmsnorm_rows(x, g_ref[...])
    h_ref[...] = h

    def win(w, c0, gw):
        acc = h[:, c0:c0 + gw]
        for dlt in range(1, w):
            acc = acc + st_ref[POOL_CTX - dlt][:, c0:c0 + gw]
        return acc

    def inv_cnt(w):
        return 1.0 / float(min(past + 1, w))

    y_ref[...] = x + _pool_groups(h, win, inv_cnt, pw_ref, sc_ref[...])


def _pool_sample(x, state_t, g, pw_bf, scale, past):
    db, d = x.shape
    return pl.pallas_call(
        functools.partial(_pool_sample_body, past=past),
        grid=(1,),
        in_specs=[_const_spec((db, d)), _const_spec(state_t.shape), _const_spec((1, d)),
                  _const_spec(pw_bf.shape), _const_spec((1, d))],
        out_specs=[pl.BlockSpec((db, d), lambda i: (0, 0))] * 2,
        out_shape=[jax.ShapeDtypeStruct((db, d), F32)] * 2,
        compiler_params=_cparams("arbitrary"),
        name="pool_sample",
    )(x, state_t, g, pw_bf, scale)


def _round_up(n, m):
    return (n + m - 1) // m * m


def kernel(x_prompt, x_sample, cache_k, cache_v, page_table, state_pool, meta_tokens, norm_mix_g, norm_ffn_g, w_qkv, q_norm_g, k_norm_g, lambda_q1, lambda_k1, lambda_q2, lambda_k2, subln_g, w_o, pool_w, pool_scale, w_up, w_down):
    b, seq, d = x_prompt.shape
    db, dec_seq, _ = x_sample.shape
    assert dec_seq == 1
    n_meta = meta_tokens.shape[0]
    n_heads, head_dim = cache_k.shape[3], cache_k.shape[5]
    v_dim = cache_v.shape[4]
    qw = n_heads * 2 * head_dim
    depth = norm_mix_g.shape[0]
    page = cache_k.shape[2]
    past = page_table.shape[1] * page
    t = seq + n_meta

    tm = ROW_TILE if t >= ROW_TILE else _round_up(t, 16)
    tp = _round_up(t, tm)

    meta = jnp.broadcast_to(meta_tokens.astype(x_prompt.dtype)[None], (b, n_meta, d))
    xp = jnp.concatenate([meta, x_prompt, jnp.zeros((b, tp - t, d), x_prompt.dtype)], axis=1)
    xs = x_sample.reshape(1, db, d)

    slopes = jnp.asarray(LOG2E * (2.0 ** (-8.0 / n_heads)) ** np.arange(1, n_heads + 1), dtype=F32)
    slope_rows = jnp.tile(slopes, 2).reshape(2 * n_heads, 1)
    blk = np.arange(MXU_DIM_V7X) // head_dim
    bd = jnp.asarray((blk[:, None] == blk[None, :]) / head_dim, dtype=BF16)

    row1 = lambda v: v.reshape(1, -1)
    kp_l, vp_l, ks_l, vs_l, sp_l, ss_l = [], [], [], [], [], []
    for i in range(depth):
        gm = row1(norm_mix_g[i])
        if i % N_MIXERS == 0:
            a = i // N_MIXERS
            w_bf = w_qkv[a].astype(BF16)
            qkv_w = (w_bf[:, 0:qw], w_bf[:, qw:2 * qw].T, w_bf[:, 2 * qw:3 * qw],
                     row1(jnp.tile(q_norm_g[a], qw // head_dim)),
                     jnp.tile(k_norm_g[a], qw // head_dim).reshape(qw, 1), bd, head_dim)
            wo_bf = w_o[a].astype(BF16)
            lam_args = (row1(lambda_q1[a]), row1(lambda_k1[a]), row1(lambda_q2[a]), row1(lambda_k2[a]),
                        row1(subln_g[a]))
            q_bf, kt_f, v_f, kt_bf, v_bf = _qkv(xp, t, gm, *qkv_w, tm)
            mix_p = (_attn_prompt(q_bf, kt_bf, v_bf, slopes, *lam_args, head_dim, v_dim, i), wo_bf)
            kp_l.append(jnp.transpose(kt_f.reshape(b, n_heads, 2, head_dim, t), (0, 4, 1, 2, 3)))
            vp_l.append(v_f.reshape(b, t, n_heads, v_dim))

            qs_bf, kst_f, vs_f, _, _ = _qkv(xs, db, gm, *qkv_w, db)
            ks_f = kst_f[0].T
            os_ = _attn_sample(qs_bf[0], ks_f, vs_f.reshape(db, n_heads, v_dim), cache_k, cache_v, a,
                               page_table, slope_rows, *lam_args, head_dim, v_dim, i, DECODE_PAGES_PER_STEP)
            mix_s = (os_[None], wo_bf)
            ks_l.append(ks_f.reshape(db, 1, n_heads, 2, head_dim))
            vs_l.append(vs_f.reshape(db, 1, n_heads, v_dim))
        else:
            p = i // N_MIXERS
            pw_bf = pool_w[p].astype(BF16)
            sc = row1(pool_scale[p])
            xp, tail = _pool_prompt(xp, t, gm, pw_bf, sc, tm)
            sp_l.append(tail[:, 1:])
            xs2, hs = _pool_sample(xs[0], jnp.swapaxes(state_pool[p], 0, 1), gm, pw_bf, sc, past)
            xs = xs2[None]
            mix_p = mix_s = None
            ss_l.append(jnp.concatenate([state_pool[p][:, 1:], hs[:, None]], axis=1))
        gf = row1(norm_ffn_g[i])
        wu_bf = w_up[i].astype(BF16)
        wd_bf = w_down[i].astype(BF16)
        if i == depth - 1:
            y_prompt = _mlp(xp, gf, wu_bf, wd_bf, tm, mix_p, shift=n_meta, out_rows=seq)
        else:
            xp = _mlp(xp, gf, wu_bf, wd_bf, tm, mix_p)
        xs = _mlp(xs, gf, wu_bf, wd_bf, db, mix_s)

    y_sample = xs.reshape(db, 1, d)
    return (y_prompt, y_sample, jnp.stack(kp_l), jnp.stack(vp_l), jnp.stack(ks_l), jnp.stack(vs_l),
            jnp.stack(sp_l), jnp.stack(ss_l))
```

```python
import functools
import math

import numpy as np
import jax
import jax.numpy as jnp
from jax import lax
from jax.experimental import pallas as pl
from jax.experimental.pallas import tpu as pltpu

F32 = jnp.float32
BF16 = jnp.bfloat16

NORM_EPS = 1e-6
POOL_WINDOWS = (2, 4, 8, 16)
POOL_CTX = max(POOL_WINDOWS) - 1
N_MIXERS = 2

LANES = 128
MXU_DIM_V7X = 256
ROW_TILE = 768
ATTN_ROW_CHUNK = 256
SOFTMAX_ROW_CHUNK = 16
FF_CHUNK = 1024
CAST_ROW_BLOCK = 512
LOG2E = math.log2(math.e)
DECODE_PAGES_PER_STEP = 16
VMEM_LIMIT_BYTES_V7X = 56 * 1024 * 1024
NEG_BIG = -1e30


def _lambda_init(layer):
    return 0.8 - 0.6 * math.exp(-0.3 * layer)


def _cparams(*sem):
    return pltpu.CompilerParams(dimension_semantics=sem, vmem_limit_bytes=VMEM_LIMIT_BYTES_V7X)


def _const_spec(shape):
    nd = len(shape)
    return pl.BlockSpec(shape, lambda *_: (0,) * nd, pipeline_mode=pl.Buffered(1))


def _rmsnorm_rows(x, g):
    ms = jnp.mean(x * x, axis=-1, keepdims=True)
    return x * lax.rsqrt(ms + NORM_EPS) * g


def _diff_lambda(lq1_ref, lk1_ref, lq2_ref, lk2_ref, layer):
    a = jnp.sum(lq1_ref[...] * lk1_ref[...], axis=-1, keepdims=True)
    b = jnp.sum(lq2_ref[...] * lk2_ref[...], axis=-1, keepdims=True)
    return jnp.exp(a) - jnp.exp(b) + _lambda_init(layer)


def _qkv_body(x_ref, g_ref, wq_ref, wkt_ref, wv_ref, qg_ref, kgc_ref, bd_ref,
              q_ref, ktf_ref, vf_ref, ktb_ref, vb_ref, *, head_dim):
    xn = _rmsnorm_rows(x_ref[0], g_ref[...]).astype(BF16)
    tm = xn.shape[0]
    qw = q_ref.shape[-1]

    q = jnp.dot(xn, wq_ref[...], preferred_element_type=F32)
    qq = (q * q).astype(BF16)
    parts = [jnp.dot(qq[:, c:c + MXU_DIM_V7X], bd_ref[...], preferred_element_type=F32)
             for c in range(0, qw, MXU_DIM_V7X)]
    q = q * lax.rsqrt(jnp.concatenate(parts, axis=-1) + NORM_EPS) * qg_ref[...]
    q_ref[0] = (q * (head_dim ** -0.5 * LOG2E)).astype(BF16)

    kt = lax.dot_general(wkt_ref[...], xn, (((1,), (1,)), ((), ())), preferred_element_type=F32)
    kt3 = kt.reshape(qw // head_dim, head_dim, tm)
    ms = jnp.mean(kt3 * kt3, axis=1, keepdims=True)
    kt = (kt3 * lax.rsqrt(ms + NORM_EPS)).reshape(qw, tm) * kgc_ref[...]
    ktf_ref[0] = kt
    ktb_ref[0, :, 0] = kt.astype(BF16).reshape(ktb_ref.shape[1], ktb_ref.shape[3], tm)

    v = jnp.dot(xn, wv_ref[...], preferred_element_type=F32)
    vf_ref[0] = v
    vb_ref[0] = v.astype(BF16)


def _qkv(x, t_valid, g, wq_bf, wkt_bf, wv_bf, qg_t, kg_col, bd, head_dim, tm):
    b, tp, d = x.shape
    qw = wq_bf.shape[1]
    hw = 2 * head_dim
    row = lambda bi, ti: (bi, ti, 0)
    return pl.pallas_call(
        functools.partial(_qkv_body, head_dim=head_dim),
        grid=(b, tp // tm),
        in_specs=[pl.BlockSpec((1, tm, d), row), _const_spec((1, d)),
                  _const_spec(wq_bf.shape), _const_spec(wkt_bf.shape), _const_spec(wv_bf.shape),
                  _const_spec((1, qw)), _const_spec((qw, 1)), _const_spec(bd.shape)],
        out_specs=[pl.BlockSpec((1, tm, qw), row),
                   pl.BlockSpec((1, qw, tm), lambda bi, ti: (bi, 0, ti)),
                   pl.BlockSpec((1, tm, qw), row),
                   pl.BlockSpec((1, qw // hw, 1, hw, tm), lambda bi, ti: (bi, 0, ti, 0, 0)),
                   pl.BlockSpec((1, tm, qw), row)],
        out_shape=[jax.ShapeDtypeStruct((b, tp, qw), BF16),
                   jax.ShapeDtypeStruct((b, qw, t_valid), F32),
                   jax.ShapeDtypeStruct((b, t_valid, qw), F32),
                   jax.ShapeDtypeStruct((b, qw // hw, tp // tm, hw, tm), BF16),
                   jax.ShapeDtypeStruct((b, tp, qw), BF16)],
        compiler_params=_cparams("parallel", "parallel"),
        name="qkv",
    )(x, g, wq_bf, wkt_bf, wv_bf, qg_t, kg_col, bd)


def _attn_body(slopes_ref, pi_ref, pj_ref, plast_ref,
               q_ref, k_ref, v_ref, lq1_ref, lk1_ref, lq2_ref, lk2_ref, sg_ref, o_ref,
               bias_ref, s0_ref, s1_ref, tm0_ref, tm1_ref, p0_ref, p1_ref, al0_ref, al1_ref, m_ref, acc_ref,
               *, t, rc, rb, n_pairs, head_dim, layer):
    slope = slopes_ref[pl.program_id(1)]
    vd = v_ref.shape[-1]
    chunks = range(0, 2 * t, rc)
    s_refs, p_refs, al_refs = (s0_ref, s1_ref), (p0_ref, p1_ref), (al0_ref, al1_ref)
    tm_refs = (tm0_ref, tm1_ref)

    rel = lax.broadcasted_iota(jnp.int32, (rc, t), 1) - lax.broadcasted_iota(jnp.int32, (rc, t), 0)
    relb = rel.astype(F32) * slope
    bias_ref[0] = relb
    for k in range(t // rc):
        bias_ref[1 + k] = jnp.where(rel - k * rc <= 0, relb, NEG_BIG)

    m_ref[...] = jnp.full(m_ref.shape, NEG_BIG, F32)
    acc_ref[...] = jnp.zeros(acc_ref.shape, F32)
    p1_ref[...] = jnp.zeros(p1_ref.shape, BF16)
    al1_ref[...] = jnp.ones(al1_ref.shape, F32)

    def scores(n, buf):
        i, j = pi_ref[n], pj_ref[n]
        kj = k_ref[0, 0, j]
        for r0 in chunks:
            qc = q_ref[0, pl.ds(pl.multiple_of(i * t + r0 % t, rc), rc), :]
            lane = lax.broadcasted_iota(jnp.int32, qc.shape, 1)
            keep = lane < head_dim if r0 < t else lane >= head_dim
            slot = jnp.where(i == j, 1 + (r0 % t) // rc, 0)
            tt = bias_ref[slot] + jnp.dot(jnp.where(keep, qc, jnp.zeros_like(qc)), kj,
                                          preferred_element_type=F32)
            s_refs[buf][r0:r0 + rc, :] = tt
            tmax = tt[:, 0:LANES]
            for c0 in range(LANES, t, LANES):
                tmax = jnp.maximum(tmax, tt[:, c0:c0 + LANES])
            tm_refs[buf][r0:r0 + rc, :] = tmax

    def softmax(n, buf):
        i, j = pi_ref[n], pj_ref[n]
        for r0 in range(0, 2 * t, rb):
            rows = slice(r0, r0 + rb)
            c = slope * ((j - i) * t - (r0 // rc * rc) % t).astype(F32)
            m_prev = jnp.where(j == 0, NEG_BIG, m_ref[rows, :])
            m_new = jnp.maximum(m_prev, jnp.max(tm_refs[buf][rows, :], axis=-1, keepdims=True) + c)
            al_refs[buf][rows, :] = jnp.exp2(m_prev - m_new)
            p_refs[buf][rows, :] = jnp.exp2(s_refs[buf][rows, :] - (m_new - c)).astype(BF16)
            m_ref[rows, :] = m_new

    def values(n, buf):
        j = pj_ref[n]
        vj = v_ref[0, pl.ds(pl.multiple_of(j * t, t), t), :]
        v1 = jnp.concatenate([vj, jnp.ones_like(vj)], axis=1)
        for r0 in chunks:
            rows = slice(r0, r0 + rc)
            acc_ref[rows, :] = (al_refs[buf][rows, :] * acc_ref[rows, :]
                                + jnp.dot(p_refs[buf][rows, :], v1, preferred_element_type=F32))

    def finish_tile(n):
        i = pi_ref[n]
        lam = _diff_lambda(lq1_ref, lk1_ref, lq2_ref, lk2_ref, layer)
        o1 = acc_ref[0:t, 0:vd] / acc_ref[0:t, vd:2 * vd]
        o2 = acc_ref[t:2 * t, 0:vd] / acc_ref[t:2 * t, vd:2 * vd]
        o = _rmsnorm_rows(o1 - lam * o2, sg_ref[...]) * (1.0 - _lambda_init(layer))
        o_ref[0, pl.ds(pl.multiple_of(i * t, t), t), :] = o.astype(o_ref.dtype)

    def step(n, buf):
        scores(jnp.minimum(n + 1, n_pairs - 1), 1 - buf)
        softmax(n, buf)
        prev = jnp.maximum(n - 1, 0)
        values(prev, 1 - buf)

        @pl.when(jnp.logical_and(n >= 1, plast_ref[prev] == 1))
        def _():
            finish_tile(prev)

    scores(0, 0)

    def two_steps(k, carry):
        step(2 * k, 0)
        step(2 * k + 1, 1)
        return carry

    lax.fori_loop(0, n_pairs // 2, two_steps, 0)
    if n_pairs % 2:
        step(n_pairs - 1, 0)
    values(n_pairs - 1, (n_pairs - 1) % 2)
    finish_tile(n_pairs - 1)


def _attn_prompt(q_bf, kt_bf, v_bf, slopes, lq1, lk1, lq2, lk2, sg, head_dim, v_dim, layer):
    b, tp, qw = q_bf.shape
    n_heads, n_t, _, t = kt_bf.shape[1:]
    assert v_dim == 2 * head_dim and n_t * t == tp and qw == n_heads * 2 * head_dim
    rc = ATTN_ROW_CHUNK if t % ATTN_ROW_CHUNK == 0 else t
    rb = SOFTMAX_ROW_CHUNK if t % SOFTMAX_ROW_CHUNK == 0 else t
    pairs = [(i, j) for i in range(n_t) for j in range(i + 1)]
    pi = jnp.asarray([p[0] for p in pairs], jnp.int32)
    pj = jnp.asarray([p[1] for p in pairs], jnp.int32)
    plast = jnp.asarray([int(p[0] == p[1]) for p in pairs], jnp.int32)
    hw = 2 * head_dim
    small = _const_spec((1, head_dim))
    per_head = pl.BlockSpec
    return pl.pallas_call(
        functools.partial(_attn_body, t=t, rc=rc, rb=rb, n_pairs=len(pairs), head_dim=head_dim, layer=layer),
        grid_spec=pltpu.PrefetchScalarGridSpec(
            num_scalar_prefetch=4,
            grid=(b, n_heads),
            in_specs=[per_head((1, tp, hw), lambda bi, hi, *_: (bi, 0, hi)),
                      per_head((1, 1, n_t, hw, t), lambda bi, hi, *_: (bi, hi, 0, 0, 0)),
                      per_head((1, tp, v_dim), lambda bi, hi, *_: (bi, 0, hi)),
                      small, small, small, small, _const_spec((1, v_dim))],
            out_specs=pl.BlockSpec((1, tp, v_dim), lambda bi, hi, *_: (bi, 0, hi)),
            scratch_shapes=[pltpu.VMEM((1 + t // rc, rc, t), F32),
                            pltpu.VMEM((2 * t, t), F32), pltpu.VMEM((2 * t, t), F32),
                            pltpu.VMEM((2 * t, LANES), F32), pltpu.VMEM((2 * t, LANES), F32),
                            pltpu.VMEM((2 * t, t), BF16), pltpu.VMEM((2 * t, t), BF16),
                            pltpu.VMEM((2 * t, 1), F32), pltpu.VMEM((2 * t, 1), F32),
                            pltpu.VMEM((2 * t, 1), F32),
                            pltpu.VMEM((2 * t, 2 * v_dim), F32)]),
        out_shape=jax.ShapeDtypeStruct((b, tp, n_heads * v_dim), BF16),
        compiler_params=_cparams("parallel", "arbitrary"),
        name="attn_prompt",
    )(slopes, pi, pj, plast, q_bf, kt_bf, v_bf, lq1, lk1, lq2, lk2, sg)


def _decode_body(pt_ref, q_ref, kn_ref, vn_ref, slope_ref, lq1_ref, lk1_ref, lq2_ref, lk2_ref, sg_ref,
                 ex_ref, *refs, pages, page, head_dim, past, layer):
    k_refs = refs[:pages]
    v_refs = refs[pages:2 * pages]
    o_ref = refs[2 * pages]
    qt_ref, m_ref, l_ref, acc_ref = refs[2 * pages + 1:]
    p_idx = pl.program_id(1)
    n_maps, width = qt_ref.shape
    n_heads = n_maps // 2

    @pl.when(p_idx == 0)
    def _init():
        q = q_ref[0].astype(F32)
        row = lax.broadcasted_iota(jnp.int32, (n_maps, width), 0)
        lane = lax.broadcasted_iota(jnp.int32, (n_maps, width), 1)
        qt = jnp.where(lane // head_dim == 2 * (row % n_heads) + row // n_heads, q, 0.0)
        qt_ref[...] = qt.astype(BF16)
        m_ref[...] = jnp.sum(qt * kn_ref[0], axis=-1, keepdims=True)
        l_ref[...] = jnp.ones(l_ref.shape, F32)
        acc_ref[...] = jnp.concatenate([vn_ref[0], vn_ref[0]], axis=0)

    qt = qt_ref[...]
    s = jnp.concatenate(
        [jnp.dot(qt, k_refs[n][...].astype(BF16), preferred_element_type=F32) for n in range(pages)],
        axis=1)
    kpos = p_idx * (pages * page) + lax.broadcasted_iota(jnp.int32, s.shape, 1)
    s = s - slope_ref[...] * (past - kpos).astype(F32)
    m_prev = m_ref[...]
    m_new = jnp.maximum(m_prev, jnp.max(s, axis=-1, keepdims=True))
    alpha = jnp.exp2(m_prev - m_new)
    p = jnp.exp2(s - m_new)
    l_ref[...] = alpha * l_ref[...] + jnp.sum(p, axis=-1, keepdims=True)
    pstack = jnp.concatenate([p[:, n * page:(n + 1) * page] for n in range(pages)], axis=0).astype(BF16)
    pexp = jnp.dot(pstack, ex_ref[...], preferred_element_type=F32)
    row = lax.broadcasted_iota(jnp.int32, pexp.shape, 0)
    lane = lax.broadcasted_iota(jnp.int32, pexp.shape, 1)
    pexp = jnp.where(lane % n_heads == row % n_heads, pexp, 0.0).astype(BF16)
    pv = jnp.dot(pexp[0:n_maps], v_refs[0][...].astype(BF16), preferred_element_type=F32)
    for n in range(1, pages):
        pv = pv + jnp.dot(pexp[n * n_maps:(n + 1) * n_maps], v_refs[n][...].astype(BF16),
                          preferred_element_type=F32)
    acc_ref[...] = alpha * acc_ref[...] + pv
    m_ref[...] = m_new

    @pl.when(p_idx == pl.num_programs(1) - 1)
    def _finish():
        lam = _diff_lambda(lq1_ref, lk1_ref, lq2_ref, lk2_ref, layer)
        accn = acc_ref[...] / l_ref[...]
        o = accn[0:n_heads] - lam * accn[n_heads:n_maps]
        o_ref[0] = _rmsnorm_rows(o, sg_ref[...]) * (1.0 - _lambda_init(layer))


def _attn_sample(q_bf, k_new, v_new, cache_k, cache_v, a, page_table, slope_rows,
                 lq1, lk1, lq2, lk2, sg, head_dim, v_dim, layer, pages):
    db, width = q_bf.shape
    n_pages = page_table.shape[1]
    n_layers, n_phys, page, n_heads = cache_k.shape[:4]
    assert n_pages % pages == 0 and width == n_heads * 2 * head_dim
    ckt = jnp.transpose(cache_k, (0, 1, 3, 4, 5, 2)).reshape(n_layers, n_phys, width, page)
    cv2 = cache_v.reshape(n_layers, n_phys, page * n_heads, v_dim)
    expand = jnp.asarray(np.arange(page * n_heads)[None, :] // n_heads == np.arange(page)[:, None], dtype=BF16)

    def page_spec(rows, cols, n):
        return pl.BlockSpec((None, None, rows, cols),
                            lambda bi, pi, pt: (a, pt[bi, pi * pages + n], 0, 0))

    rowv = pl.BlockSpec((1, 1, width), lambda bi, pi, pt: (bi, 0, 0))
    small = _const_spec((1, head_dim))
    out = pl.pallas_call(
        functools.partial(_decode_body, pages=pages, page=page, head_dim=head_dim,
                          past=n_pages * page, layer=layer),
        grid_spec=pltpu.PrefetchScalarGridSpec(
            num_scalar_prefetch=1,
            grid=(db, n_pages // pages),
            in_specs=[rowv, rowv, pl.BlockSpec((1, n_heads, v_dim), lambda bi, pi, pt: (bi, 0, 0)),
                      _const_spec((2 * n_heads, 1)), small, small, small, small, _const_spec((1, v_dim)),
                      _const_spec(expand.shape)]
                     + [page_spec(width, page, n) for n in range(pages)]
                     + [page_spec(page * n_heads, v_dim, n) for n in range(pages)],
            out_specs=pl.BlockSpec((1, n_heads, v_dim), lambda bi, pi, pt: (bi, 0, 0)),
            scratch_shapes=[pltpu.VMEM((2 * n_heads, width), BF16),
                            pltpu.VMEM((2 * n_heads, 1), F32),
                            pltpu.VMEM((2 * n_heads, 1), F32),
                            pltpu.VMEM((2 * n_heads, v_dim), F32)]),
        out_shape=jax.ShapeDtypeStruct((db, n_heads, v_dim), F32),
        compiler_params=_cparams("parallel", "arbitrary"),
        name="attn_sample",
    )(page_table, q_bf.reshape(db, 1, width), k_new.reshape(db, 1, width), v_new,
      slope_rows, lq1, lk1, lq2, lk2, sg, expand, *([ckt] * pages), *([cv2] * pages))
    return out.reshape(db, n_heads * v_dim)


def _mlp_body(*refs, ff_chunk, shift, has_mix):
    refs = list(refs)

    def take_rows():
        a = refs.pop(0)[0]
        if shift:
            a = jnp.concatenate([a[shift:], refs.pop(0)[0]], axis=0)
        return a

    x = take_rows()
    if has_mix:
        o = take_rows()
        x = x + jnp.dot(o.astype(BF16), refs.pop(0)[...], preferred_element_type=F32)
    g_ref, wu_ref, wd_ref, y_ref = refs
    xn = _rmsnorm_rows(x, g_ref[...]).astype(BF16)
    acc = x
    for c in range(0, wu_ref.shape[1], ff_chunk):
        hcol = jnp.dot(xn, wu_ref[:, c:c + ff_chunk], preferred_element_type=F32)
        hcol = jnp.maximum(hcol, 0.0)
        acc = acc + jnp.dot((hcol * hcol).astype(BF16), wd_ref[c:c + ff_chunk, :],
                            preferred_element_type=F32)
    y_ref[0] = acc


def _mlp(x, g, wu_bf, wd_bf, tm, mix=None, shift=0, out_rows=None):
    b, tp, d = x.shape
    out_rows = tp if out_rows is None else out_rows
    assert shift == 0 or (shift % 8 == 0 and tm % shift == 0 and tp % shift == 0)
    row = lambda bi, ti: (bi, ti, 0)
    nxt = lambda bi, ti: (bi, jnp.minimum((ti + 1) * (tm // shift), tp // shift - 1), 0)

    def row_specs(width):
        specs = [pl.BlockSpec((1, tm, width), row)]
        return specs + [pl.BlockSpec((1, shift, width), nxt)] if shift else specs

    operands = [x, x] if shift else [x]
    in_specs = row_specs(d)
    if mix is not None:
        o, wo_bf = mix
        operands += [o, o, wo_bf] if shift else [o, wo_bf]
        in_specs += row_specs(o.shape[-1]) + [_const_spec(wo_bf.shape)]
    return pl.pallas_call(
        functools.partial(_mlp_body, ff_chunk=min(FF_CHUNK, wu_bf.shape[1]), shift=shift,
                          has_mix=mix is not None),
        grid=(b, pl.cdiv(out_rows, tm)),
        in_specs=in_specs + [_const_spec((1, d)), _const_spec(wu_bf.shape), _const_spec(wd_bf.shape)],
        out_specs=pl.BlockSpec((1, tm, d), row),
        out_shape=jax.ShapeDtypeStruct((b, out_rows, d), F32),
        compiler_params=_cparams("parallel", "parallel"),
        name="mlp",
    )(*operands, g, wu_bf, wd_bf)


def _pool_groups(h, win_fn, inv_cnt_fn, pw_ref, scale):
    gw = pw_ref.shape[-1]
    outs = []
    for gi, w in enumerate(POOL_WINDOWS):
        c0 = gi * gw
        mean = win_fn(w, c0, gw) * inv_cnt_fn(w)
        d = (mean - h[:, c0:c0 + gw]).astype(BF16)
        outs.append(jnp.dot(d, pw_ref[gi], preferred_element_type=F32))
    return jnp.concatenate(outs, axis=-1) * scale


def _pool_prompt_body(x_ref, g_ref, pw_ref, sc_ref, y_ref, tail_ref, *ext_refs, tm, tail_tile, tail_off):
    t = pl.program_id(1)
    ctx = POOL_CTX + 1
    hext_ref = ext_refs[0]
    d = hext_ref.shape[1]
    gw = pw_ref.shape[-1]

    @pl.when(t == 0)
    def _zero_ctx():
        for ref in ext_refs:
            ref[0:ctx, :] = jnp.zeros((ctx, ref.shape[1]), F32)

    @pl.when(t > 0)
    def _carry_ctx():
        for ref in ext_refs:
            ref[0:ctx, :] = ref[tm:tm + ctx, :]

    x = x_ref[0]
    h = _rmsnorm_rows(x, g_ref[...])
    hext_ref[ctx:ctx + tm, :] = h

    sums = {}
    cur = h
    for k, ref in enumerate(ext_refs):
        w = 2 ** k
        cur = cur + ref[ctx - w:ctx - w + tm, :]
        sums[2 * w] = cur[:, 0:gw]
        cur = cur[:, gw:]
        if k + 1 < len(ext_refs):
            ext_refs[k + 1][ctx:ctx + tm, :] = cur

    def win(w, c0, gw_):
        return sums[w]

    pos = t * tm + lax.broadcasted_iota(jnp.int32, (tm, 1), 0)

    def inv_cnt(w):
        return 1.0 / jnp.minimum(pos + 1, w).astype(F32)

    y_ref[0] = x + _pool_groups(h, win, inv_cnt, pw_ref, sc_ref[...])

    @pl.when(t == tail_tile)
    def _tail():
        tail_ref[0] = hext_ref[ctx + tail_off:ctx + tail_off + ctx, :]


def _pool_prompt(x, t_valid, g, pw_bf, scale, tm):
    b, tp, d = x.shape
    ctx = POOL_CTX + 1
    tail_tile, tail_off = divmod(t_valid - ctx, tm)
    assert tail_off + ctx <= tm
    assert POOL_WINDOWS == tuple(2 ** (k + 1) for k in range(len(POOL_WINDOWS)))
    gw = pw_bf.shape[-1]
    row = lambda bi, ti: (bi, ti, 0)
    return pl.pallas_call(
        functools.partial(_pool_prompt_body, tm=tm, tail_tile=tail_tile, tail_off=tail_off),
        grid=(b, tp // tm),
        in_specs=[pl.BlockSpec((1, tm, d), row), _const_spec((1, d)),
                  _const_spec(pw_bf.shape), _const_spec((1, d))],
        out_specs=[pl.BlockSpec((1, tm, d), row), pl.BlockSpec((1, ctx, d), lambda bi, ti: (bi, 0, 0))],
        out_shape=[jax.ShapeDtypeStruct((b, tp, d), F32), jax.ShapeDtypeStruct((b, ctx, d), F32)],
        scratch_shapes=[pltpu.VMEM((tm + ctx, d - k * gw), F32) for k in range(len(POOL_WINDOWS))],
        compiler_params=_cparams("parallel", "arbitrary"),
        name="pool_prompt",
    )(x, g, pw_bf, scale)


def _pool_sample_body(x_ref, st_ref, g_ref, pw_ref, sc_ref, y_ref, h_ref, *, past):
    x = x_ref[...]
    h = _rmsnorm_rows(x, g_ref[...])
    h_ref[...] = h

    def win(w, c0, gw):
        acc = h[:, c0:c0 + gw]
        for dlt in range(1, w):
            acc = acc + st_ref[POOL_CTX - dlt][:, c0:c0 + gw]
        return acc

    def inv_cnt(w):
        return 1.0 / float(min(past + 1, w))

    y_ref[...] = x + _pool_groups(h, win, inv_cnt, pw_ref, sc_ref[...])


def _pool_sample(x, state_t, g, pw_bf, scale, past):
    db, d = x.shape
    return pl.pallas_call(
        functools.partial(_pool_sample_body, past=past),
        grid=(1,),
        in_specs=[_const_spec((db, d)), _const_spec(state_t.shape), _const_spec((1, d)),
                  _const_spec(pw_bf.shape), _const_spec((1, d))],
        out_specs=[pl.BlockSpec((db, d), lambda i: (0, 0))] * 2,
        out_shape=[jax.ShapeDtypeStruct((db, d), F32)] * 2,
        compiler_params=_cparams("arbitrary"),
        name="pool_sample",
    )(x, state_t, g, pw_bf, scale)


def _round_up(n, m):
    return (n + m - 1) // m * m


def _cast_body(w_ref, o_ref):
    o_ref[...] = w_ref[...].astype(o_ref.dtype)


def _layer_bf16(w, layer):
    _, r, c = w.shape
    rb = CAST_ROW_BLOCK if r % CAST_ROW_BLOCK == 0 else r
    return pl.pallas_call(
        _cast_body,
        grid=(r // rb,),
        in_specs=[pl.BlockSpec((None, rb, c), lambda i: (layer, i, 0))],
        out_specs=pl.BlockSpec((rb, c), lambda i: (i, 0)),
        out_shape=jax.ShapeDtypeStruct((r, c), BF16),
        compiler_params=_cparams("parallel"),
        name="to_bf16",
    )(w)


def kernel(x_prompt, x_sample, cache_k, cache_v, page_table, state_pool, meta_tokens, norm_mix_g, norm_ffn_g, w_qkv, q_norm_g, k_norm_g, lambda_q1, lambda_k1, lambda_q2, lambda_k2, subln_g, w_o, pool_w, pool_scale, w_up, w_down):
    b, seq, d = x_prompt.shape
    db, dec_seq, _ = x_sample.shape
    assert dec_seq == 1
    n_meta = meta_tokens.shape[0]
    n_heads, head_dim = cache_k.shape[3], cache_k.shape[5]
    v_dim = cache_v.shape[4]
    qw = n_heads * 2 * head_dim
    depth = norm_mix_g.shape[0]
    page = cache_k.shape[2]
    past = page_table.shape[1] * page
    t = seq + n_meta

    tm = ROW_TILE if t >= ROW_TILE else _round_up(t, 16)
    tp = _round_up(t, tm)

    meta = jnp.broadcast_to(meta_tokens.astype(x_prompt.dtype)[None], (b, n_meta, d))
    xp = jnp.concatenate([meta, x_prompt, jnp.zeros((b, tp - t, d), x_prompt.dtype)], axis=1)
    xs = x_sample.reshape(1, db, d)

    slopes = jnp.asarray(LOG2E * (2.0 ** (-8.0 / n_heads)) ** np.arange(1, n_heads + 1), dtype=F32)
    slope_rows = jnp.tile(slopes, 2).reshape(2 * n_heads, 1)
    blk = np.arange(MXU_DIM_V7X) // head_dim
    bd = jnp.asarray((blk[:, None] == blk[None, :]) / head_dim, dtype=BF16)

    row1 = lambda v: v.reshape(1, -1)
    kp_l, vp_l, ks_l, vs_l, sp_l, ss_l = [], [], [], [], [], []
    for i in range(depth):
        gm = row1(norm_mix_g[i])
        if i % N_MIXERS == 0:
            a = i // N_MIXERS
            w_bf = _layer_bf16(w_qkv, a)
            qkv_w = (w_bf[:, 0:qw], w_bf[:, qw:2 * qw].T, w_bf[:, 2 * qw:3 * qw],
                     row1(jnp.tile(q_norm_g[a], qw // head_dim)),
                     jnp.tile(k_norm_g[a], qw // head_dim).reshape(qw, 1), bd, head_dim)
            wo_bf = _layer_bf16(w_o, a)
            lam_args = (row1(lambda_q1[a]), row1(lambda_k1[a]), row1(lambda_q2[a]), row1(lambda_k2[a]),
                        row1(subln_g[a]))
            q_bf, kt_f, v_f, kt_bf, v_bf = _qkv(xp, t, gm, *qkv_w, tm)
            mix_p = (_attn_prompt(q_bf, kt_bf, v_bf, slopes, *lam_args, head_dim, v_dim, i), wo_bf)
            kp_l.append(jnp.transpose(kt_f.reshape(b, n_heads, 2, head_dim, t), (0, 4, 1, 2, 3)))
            vp_l.append(v_f.reshape(b, t, n_heads, v_dim))

            qs_bf, kst_f, vs_f, _, _ = _qkv(xs, db, gm, *qkv_w, db)
            ks_f = kst_f[0].T
            os_ = _attn_sample(qs_bf[0], ks_f, vs_f.reshape(db, n_heads, v_dim), cache_k, cache_v, a,
                               page_table, slope_rows, *lam_args, head_dim, v_dim, i, DECODE_PAGES_PER_STEP)
            mix_s = (os_[None], wo_bf)
            ks_l.append(ks_f.reshape(db, 1, n_heads, 2, head_dim))
            vs_l.append(vs_f.reshape(db, 1, n_heads, v_dim))
        else:
            p = i // N_MIXERS
            n_grp, gw = pool_w.shape[1:3]
            pw_bf = _layer_bf16(pool_w.reshape(-1, n_grp * gw, gw), p).reshape(n_grp, gw, gw)
            sc = row1(pool_scale[p])
            xp, tail = _pool_prompt(xp, t, gm, pw_bf, sc, tm)
            sp_l.append(tail[:, 1:])
            xs2, hs = _pool_sample(xs[0], jnp.swapaxes(state_pool[p], 0, 1), gm, pw_bf, sc, past)
            xs = xs2[None]
            mix_p = mix_s = None
            ss_l.append(jnp.concatenate([state_pool[p][:, 1:], hs[:, None]], axis=1))
        gf = row1(norm_ffn_g[i])
        wu_bf = _layer_bf16(w_up, i)
        wd_bf = _layer_bf16(w_down, i)
        if i == depth - 1:
            y_prompt = _mlp(xp, gf, wu_bf, wd_bf, tm, mix_p, shift=n_meta, out_rows=seq)
        else:
            xp = _mlp(xp, gf, wu_bf, wd_bf, tm, mix_p)
        xs = _mlp(xs, gf, wu_bf, wd_bf, db, mix_s)

    y_sample = xs.reshape(db, 1, d)
    return (y_prompt, y_sample, jnp.stack(kp_l), jnp.stack(vp_l), jnp.stack(ks_l), jnp.stack(vs_l),
            jnp.stack(sp_l), jnp.stack(ss_l))
```
